```python
import math
import jax
import jax.numpy as jnp
from jax import lax
import numpy as np

D_MODEL = 1024
BATCH = 16
SEQ = 2048
DEPTH = 4

GRID_W = 64
CTX_LEN = 256

DA_HEADS = 4
DA_HEAD_DIM = 48
DA_V_DIM = 2 * DA_HEAD_DIM
DA_WIDTH = DA_HEADS * DA_V_DIM
DA_IN = 3 * DA_HEADS * 2 * DA_HEAD_DIM

HY_CH = 256
HY_ORDER = 2
HY_SHORT = 3
HY_EMB = 33
HY_HIDDEN = 64
HY_DECAY_TARGET = 1e-2
HY_DECAY_SHORT_PCT = 0.3
HY_DECAY_LONG_PCT = 1.5
HY_DECAY_SHIFT = 0.05
HY_IN = (HY_ORDER + 1) * HY_CH

MLA_HEADS = 6
MLA_Q_RANK = 256
MLA_KV_RANK = 128
MLA_NOPE = 64
MLA_ROPE = 32
MLA_V = 64
MLA_WIDTH = MLA_HEADS * MLA_V
MLA_IN = MLA_Q_RANK + MLA_KV_RANK + MLA_ROPE

IN_WIDTH = DA_IN + HY_IN + MLA_IN
MIX_WIDTH = DA_WIDTH + HY_CH + MLA_WIDTH

FFN_DIM = 2816
N_EXPERTS = 8
TOP_K = 2
EXPERT_DIM = 2816
MOE_BLOCK = 256

ROPE_BASE = 10000.0
Q_BLOCK = 128
LN_EPS = 1e-5
RMS_EPS = 1e-6
DEEPNORM_ALPHA = (2 * DEPTH) ** 0.25
DEEPNORM_BETA = (8 * DEPTH) ** -0.25

kernel_name = 'hybrid_diffattn_hyena_mla_moe_trunk'


def _layer_norm(x, g, b):
    xf = x.astype(jnp.float32)
    mu = jnp.mean(xf, axis=-1, keepdims=True)
    var = jnp.mean(jnp.square(xf - mu), axis=-1, keepdims=True)
    return ((xf - mu) * lax.rsqrt(var + LN_EPS) * g + b).astype(x.dtype)


def _rms_norm(x, g):
    xf = x.astype(jnp.float32)
    return (xf * lax.rsqrt(jnp.mean(xf * xf, axis=-1, keepdims=True) + RMS_EPS) * g).astype(x.dtype)


def _modulate(h, shift, scale):
    return h * (1.0 + scale) + shift


def _axial_rope_tables(n_tokens, rot_dim):
    rows = n_tokens // GRID_W
    row = jnp.broadcast_to(jnp.arange(rows)[:, None], (rows, GRID_W)).reshape(-1).astype(jnp.float32)
    col = jnp.broadcast_to(jnp.arange(GRID_W)[None, :], (rows, GRID_W)).reshape(-1).astype(jnp.float32)
    axis_dim = rot_dim // 2
    inv_freq = ROPE_BASE ** (-jnp.arange(0, axis_dim, 2, dtype=jnp.float32) / axis_dim)

    def table(pos):
        ang = pos[:, None] * inv_freq[None, :]
        ang = jnp.concatenate([ang, ang], axis=-1)
        return jnp.cos(ang), jnp.sin(ang)

    cos_r, sin_r = table(row)
    cos_c, sin_c = table(col)
    return cos_r, sin_r, cos_c, sin_c


def _rotate_half(x):
    x1, x2 = jnp.split(x, 2, axis=-1)
    return jnp.concatenate([-x2, x1], axis=-1)


def _apply_axial_rope(x, tables):
    cos_r, sin_r, cos_c, sin_c = tables
    xr, xc = jnp.split(x.astype(jnp.float32), 2, axis=-1)
    out = jnp.concatenate([xr * cos_r + _rotate_half(xr) * sin_r,
                           xc * cos_c + _rotate_half(xc) * sin_c], axis=-1)
    return out.astype(x.dtype)


def _attn_probs(q, k, scale):
    s = jnp.einsum('bhqd,bhkd->bhqk', q, k).astype(jnp.float32) * scale
    return jax.nn.softmax(s, axis=-1)


def _softmax_attend(q, k, v, scale):
    p = _attn_probs(q, k, scale)
    return jnp.einsum('bhqk,bhkd->bhqd', p.astype(v.dtype), v)


def _sweep_query_blocks(fn, qs):
    b, h, n, _ = qs[0].shape
    nb = n // Q_BLOCK
    blocks = tuple(q.reshape(b, h, nb, Q_BLOCK, q.shape[-1]).transpose(2, 0, 1, 3, 4) for q in qs)
    out = lax.map(lambda qb: fn(*qb), blocks)
    return out.transpose(1, 2, 0, 3, 4).reshape(b, h, n, out.shape[-1])


def _diff_attention(p_ctx, p_lat, rope, lam_q1, lam_k1, lam_q2, lam_k2, subln_g, lambda_init, need_ctx):
    def split_heads(p):
        b, n, _ = p.shape
        q, k, v = jnp.split(p, 3, axis=-1)
        q = q.reshape(b, n, DA_HEADS, 2, DA_HEAD_DIM).transpose(3, 0, 2, 1, 4)
        k = k.reshape(b, n, DA_HEADS, 2, DA_HEAD_DIM).transpose(3, 0, 2, 1, 4)
        v = v.reshape(b, n, DA_HEADS, DA_V_DIM).transpose(0, 2, 1, 3)
        return q, k, v

    q_c, k_c, v_c = split_heads(p_ctx)
    q_l, k_l, v_l = split_heads(p_lat)
    q_l = _apply_axial_rope(q_l, rope)
    k_l = _apply_axial_rope(k_l, rope)
    f32 = jnp.float32
    lam = (jnp.exp(jnp.sum(lam_q1.astype(f32) * lam_k1.astype(f32)))
           - jnp.exp(jnp.sum(lam_q2.astype(f32) * lam_k2.astype(f32))) + lambda_init)
    scale = DA_HEAD_DIM ** -0.5

    def attend(q1, q2, k1, k2, v):
        a = _attn_probs(q1, k1, scale) - lam * _attn_probs(q2, k2, scale)
        return jnp.einsum('bhqk,bhkd->bhqd', a.astype(v.dtype), v)

    def merge(o):
        b, h, n, dv = o.shape
        o = _rms_norm(o, subln_g) * (1.0 - lambda_init)
        return o.transpose(0, 2, 1, 3).reshape(b, n, h * dv)

    k_all = jnp.concatenate([k_c, k_l], axis=3)
    v_all = jnp.concatenate([v_c, v_l], axis=2)
    o_lat = _sweep_query_blocks(lambda q1, q2: attend(q1, q2, k_all[0], k_all[1], v_all), (q_l[0], q_l[1]))
    o_ctx = merge(attend(q_c[0], q_c[1], k_c[0], k_c[1], v_c)) if need_ctx else None
    return o_ctx, merge(o_lat)


def _hyena_filter_spectrum(n, fw1, fb1, fw2, fb2, fw3, fb3, fw4, freq):
    f32 = jnp.float32
    t = jnp.linspace(0.0, 1.0, n, dtype=f32)[:, None]
    bands = (HY_EMB - 1) // 2
    w = 2.0 * math.pi * jnp.arange(n, dtype=f32)[:, None] / n
    f = jnp.linspace(1e-4, bands - 1, bands, dtype=f32)[None, :]
    z = jnp.concatenate([t, jnp.cos(f * w), -jnp.sin(f * w)], axis=-1)
    fr = freq.astype(f32)
    hdn = jnp.sin(fr * (z @ fw1.astype(f32) + fb1.astype(f32)))
    hdn = jnp.sin(fr * (hdn @ fw2.astype(f32) + fb2.astype(f32)))
    hdn = jnp.sin(fr * (hdn @ fw3.astype(f32) + fb3.astype(f32)))
    filt = (hdn @ fw4.astype(f32)).reshape(n, 2, HY_ORDER, HY_CH)
    max_decay = math.log(HY_DECAY_TARGET) / HY_DECAY_SHORT_PCT
    min_decay = math.log(HY_DECAY_TARGET) / HY_DECAY_LONG_PCT
    deltas = jnp.linspace(min_decay, max_decay, HY_CH, dtype=f32)
    window = jnp.exp(-t * jnp.abs(deltas)[None, :]) + HY_DECAY_SHIFT
    filt = filt * window[:, None, None, :]
    fwd, bwd = filt[:, 0], filt[:, 1]
    circ = jnp.concatenate([fwd, jnp.zeros((1, HY_ORDER, HY_CH), f32), bwd[:0:-1]], axis=0)
    return jnp.fft.rfft(circ, axis=0)


def _long_conv(z, spec, bias):
    n = z.shape[1]
    zf = z.astype(jnp.float32)
    y = jnp.fft.irfft(jnp.fft.rfft(zf, n=2 * n, axis=1) * spec[None], n=2 * n, axis=1)[:, :n]
    return (y + zf * bias.astype(jnp.float32)).astype(z.dtype)


def _hyena(p, conv_w, conv_b, fw1, fb1, fw2, fb2, fw3, fb3, fw4, freq, bias):
    n = p.shape[1]
    u = lax.conv_general_dilated(p, conv_w[:, None, :], window_strides=(1,),
                                 padding=((HY_SHORT // 2, HY_SHORT // 2),),
                                 dimension_numbers=('NWC', 'WIO', 'NWC'),
                                 feature_group_count=p.shape[-1]) + conv_b
    parts = jnp.split(u, HY_ORDER + 1, axis=-1)
    gates, z = parts[:-1], parts[-1]
    spec = _hyena_filter_spectrum(n, fw1, fb1, fw2, fb2, fw3, fb3, fw4, freq)
    for i in range(HY_ORDER):
        z = gates[i] * _long_conv(z, spec[:, i], bias[i])
    return z


def _mla(p_ctx, p_lat, rope, q_norm_g, w_uq, kv_norm_g, w_ukv, need_ctx):
    def project(p, rotate):
        b, n, _ = p.shape
        c_q, c_kv, k_rope = jnp.split(p, [MLA_Q_RANK, MLA_Q_RANK + MLA_KV_RANK], axis=-1)
        q = (_rms_norm(c_q, q_norm_g) @ w_uq).reshape(b, n, MLA_HEADS, MLA_NOPE + MLA_ROPE).transpose(0, 2, 1, 3)
        kv = (_rms_norm(c_kv, kv_norm_g) @ w_ukv).reshape(b, n, MLA_HEADS, MLA_NOPE + MLA_V).transpose(0, 2, 1, 3)
        q_nope, q_rope = jnp.split(q, [MLA_NOPE], axis=-1)
        k_nope, v = jnp.split(kv, [MLA_NOPE], axis=-1)
        k_rope = k_rope[:, None]
        if rotate:
            q_rope = _apply_axial_rope(q_rope, rope)
            k_rope = _apply_axial_rope(k_rope, rope)
        q = jnp.concatenate([q_nope, q_rope], axis=-1)
        k = jnp.concatenate([k_nope, jnp.broadcast_to(k_rope, (b, MLA_HEADS, n, MLA_ROPE))], axis=-1)
        return q, k, v

    def merge(o):
        b, h, n, dv = o.shape
        return o.transpose(0, 2, 1, 3).reshape(b, n, h * dv)

    scale = (MLA_NOPE + MLA_ROPE) ** -0.5
    q_c, k_c, v_c = project(p_ctx, False)
    q_l, k_l, v_l = project(p_lat, True)
    k_all = jnp.concatenate([k_c, k_l], axis=2)
    v_all = jnp.concatenate([v_c, v_l], axis=2)
    o_lat = _sweep_query_blocks(lambda q: _softmax_attend(q, k_all, v_all, scale), (q_l,))
    o_ctx = merge(_softmax_attend(q_c, k_c, v_c, scale)) if need_ctx else None
    return o_ctx, merge(o_lat)


def _swiglu(h, w_gate, w_up, w_down):
    return (jax.nn.silu(h @ w_gate) * (h @ w_up)) @ w_down


def _moe_swiglu(h, router_w, w_gate, w_up, w_down):
    b, n, d = h.shape
    tokens = h.reshape(b * n, d)
    n_tok = b * n
    n_assign = n_tok * TOP_K
    logits = (tokens @ router_w).astype(jnp.float32)
    top_logits, top_idx = lax.top_k(logits, TOP_K)
    gates = jax.nn.softmax(top_logits, axis=-1).reshape(n_assign)
    expert_of = top_idx.reshape(n_assign)
    order = jnp.argsort(expert_of)
    sorted_expert = expert_of[order]
    token_of = order // TOP_K
    counts = jnp.bincount(expert_of, length=N_EXPERTS)
    padded = (counts + MOE_BLOCK - 1) // MOE_BLOCK * MOE_BLOCK
    padded_end = jnp.cumsum(padded)
    padded_start = padded_end - padded
    group_start = jnp.cumsum(counts) - counts
    dest = padded_start[sorted_expert] + jnp.arange(n_assign) - group_start[sorted_expert]
    n_blocks = -(-(n_assign + N_EXPERTS * (MOE_BLOCK - 1)) // MOE_BLOCK)
    rows = jnp.zeros((n_blocks * MOE_BLOCK, d), h.dtype).at[dest].set(tokens[token_of])
    block_expert = jnp.minimum(
        jnp.searchsorted(padded_end, jnp.arange(n_blocks) * MOE_BLOCK, side='right'), N_EXPERTS - 1)

    def expert_block(args):
        xb, e = args
        return _swiglu(xb, w_gate[e], w_up[e], w_down[e])

    out = lax.map(expert_block, (rows.reshape(n_blocks, MOE_BLOCK, d), block_expert)).reshape(-1, d)
    contrib = out[dest] * gates[order][:, None].astype(h.dtype)
    y = jnp.zeros((n_tok, d), h.dtype).at[token_of].add(contrib)
    return y.reshape(b, n, d)


def setup_inputs(seed: int = 0) -> dict:
    key = jax.random.key(seed)
    keys = iter(jax.random.split(key, 64))
    f32 = jnp.float32
    D = D_MODEL
    n_dense = (DEPTH + 1) // 2
    n_moe = DEPTH // 2

    def normal(shape, scale):
        return jax.random.normal(next(keys), shape, f32) * scale

    def gain(shape):
        return 1.0 + normal(shape, 0.05)

    return {
        'x': normal((BATCH, SEQ, D), 1.0),
        'c': normal((BATCH, D), 1.0),
        'ctx': normal((BATCH, CTX_LEN, D), 1.0),
        'c_ctx': normal((D,), 1.0),
        'ada_w': normal((DEPTH, D, 6 * D), 0.5 * D ** -0.5),
        'ada_b': normal((DEPTH, 6 * D), 0.02),
        'w_in': normal((DEPTH, D, IN_WIDTH), D ** -0.5),
        'da_lambda_q1': normal((DEPTH, DA_HEAD_DIM), 0.1),
        'da_lambda_k1': normal((DEPTH, DA_HEAD_DIM), 0.1),
        'da_lambda_q2': normal((DEPTH, DA_HEAD_DIM), 0.1),
        'da_lambda_k2': normal((DEPTH, DA_HEAD_DIM), 0.1),
        'da_subln_g': gain((DEPTH, DA_V_DIM)),
        'hy_conv_w': normal((DEPTH, HY_SHORT, HY_IN), HY_SHORT ** -0.5),
        'hy_conv_b': normal((DEPTH, HY_IN), 0.02),
        'hy_fw1': normal((DEPTH, HY_EMB, HY_HIDDEN), HY_EMB ** -0.5),
        'hy_fb1': normal((DEPTH, HY_HIDDEN), 0.02),
        'hy_fw2': normal((DEPTH, HY_HIDDEN, HY_HIDDEN), HY_HIDDEN ** -0.5),
        'hy_fb2': normal((DEPTH, HY_HIDDEN), 0.02),
        'hy_fw3': normal((DEPTH, HY_HIDDEN, HY_HIDDEN), HY_HIDDEN ** -0.5),
        'hy_fb3': normal((DEPTH, HY_HIDDEN), 0.02),
        'hy_fw4': normal((DEPTH, HY_HIDDEN, 2 * HY_ORDER * HY_CH), 0.05 * HY_HIDDEN ** -0.5),
        'hy_freq': gain((DEPTH, HY_HIDDEN)),
        'hy_bias': normal((DEPTH, HY_ORDER, HY_CH), 0.1),
        'mla_q_norm_g': gain((DEPTH, MLA_Q_RANK)),
        'mla_w_uq': normal((DEPTH, MLA_Q_RANK, MLA_HEADS * (MLA_NOPE + MLA_ROPE)), MLA_Q_RANK ** -0.5),
        'mla_kv_norm_g': gain((DEPTH, MLA_KV_RANK)),
        'mla_w_ukv': normal((DEPTH, MLA_KV_RANK, MLA_HEADS * (MLA_NOPE + MLA_V)), MLA_KV_RANK ** -0.5),
        'w_out': normal((DEPTH, MIX_WIDTH, D), DEEPNORM_BETA * MIX_WIDTH ** -0.5),
        'ln1_g': gain((DEPTH, D)),
        'ln1_b': normal((DEPTH, D), 0.02),
        'ln2_g': gain((DEPTH, D)),
        'ln2_b': normal((DEPTH, D), 0.02),
        'ffn_w_gate': normal((n_dense, D, FFN_DIM), D ** -0.5),
        'ffn_w_up': normal((n_dense, D, FFN_DIM), D ** -0.5),
        'ffn_w_down': normal((n_dense, FFN_DIM, D), DEEPNORM_BETA * FFN_DIM ** -0.5),
        'moe_router': normal((n_moe, D, N_EXPERTS), D ** -0.5),
        'moe_w_gate': normal((n_moe, N_EXPERTS, D, EXPERT_DIM), D ** -0.5),
        'moe_w_up': normal((n_moe, N_EXPERTS, D, EXPERT_DIM), D ** -0.5),
        'moe_w_down': normal((n_moe, N_EXPERTS, EXPERT_DIM, D), DEEPNORM_BETA * EXPERT_DIM ** -0.5),
    }


def reference(x, c, ctx, c_ctx, ada_w, ada_b, w_in, da_lambda_q1, da_lambda_k1, da_lambda_q2,
              da_lambda_k2, da_subln_g, hy_conv_w, hy_conv_b, hy_fw1, hy_fb1, hy_fw2, hy_fb2, hy_fw3,
              hy_fb3, hy_fw4, hy_freq, hy_bias, mla_q_norm_g, mla_w_uq, mla_kv_norm_g, mla_w_ukv, w_out,
              ln1_g, ln1_b, ln2_g, ln2_b, ffn_w_gate, ffn_w_up, ffn_w_down, moe_router, moe_w_gate,
              moe_w_up, moe_w_down):
    n_lat = x.shape[1]
    n_ctx = ctx.shape[1]
    rope_da = _axial_rope_tables(n_lat, DA_HEAD_DIM)
    rope_mla = _axial_rope_tables(n_lat, MLA_ROPE)
    c_act = jax.nn.silu(c)
    c_ctx_act = jax.nn.silu(c_ctx)
    for layer in range(DEPTH):
        need_ctx = layer < DEPTH - 1
        mod_lat = jnp.split((c_act @ ada_w[layer] + ada_b[layer])[:, None, :], 6, axis=-1)
        mod_ctx = jnp.split(c_ctx_act @ ada_w[layer] + ada_b[layer], 6, axis=-1)

        h = jnp.concatenate([_modulate(ctx, mod_ctx[0], mod_ctx[1]),
                             _modulate(x, mod_lat[0], mod_lat[1])], axis=1)
        proj = h @ w_in[layer]
        da_c, hy_c, mla_c = jnp.split(proj[:, :n_ctx], [DA_IN, DA_IN + HY_IN], axis=-1)
        da_l, hy_l, mla_l = jnp.split(proj[:, n_ctx:], [DA_IN, DA_IN + HY_IN], axis=-1)
        lambda_init = 0.8 - 0.6 * math.exp(-0.3 * layer)
        da_out_c, da_out_l = _diff_attention(da_c, da_l, rope_da, da_lambda_q1[layer], da_lambda_k1[layer],
                                             da_lambda_q2[layer], da_lambda_k2[layer], da_subln_g[layer],
                                             lambda_init, need_ctx)
        hy_params = (hy_conv_w[layer], hy_conv_b[layer], hy_fw1[layer], hy_fb1[layer], hy_fw2[layer],
                     hy_fb2[layer], hy_fw3[layer], hy_fb3[layer], hy_fw4[layer], hy_freq[layer], hy_bias[layer])
        hy_out_l = _hyena(hy_l, *hy_params)
        mla_out_c, mla_out_l = _mla(mla_c, mla_l, rope_mla, mla_q_norm_g[layer], mla_w_uq[layer],
                                    mla_kv_norm_g[layer], mla_w_ukv[layer], need_ctx)
        o_lat = jnp.concatenate([da_out_l, hy_out_l, mla_out_l], axis=-1) @ w_out[layer]
        x = _layer_norm(DEEPNORM_ALPHA * x + mod_lat[2] * o_lat, ln1_g[layer], ln1_b[layer])
        if need_ctx:
            hy_out_c = _hyena(hy_c, *hy_params)
            o_ctx = jnp.concatenate([da_out_c, hy_out_c, mla_out_c], axis=-1) @ w_out[layer]
            ctx = _layer_norm(DEEPNORM_ALPHA * ctx + mod_ctx[2] * o_ctx, ln1_g[layer], ln1_b[layer])
            h2 = jnp.concatenate([_modulate(ctx, mod_ctx[3], mod_ctx[4]),
                                  _modulate(x, mod_lat[3], mod_lat[4])], axis=1)
        else:
            h2 = _modulate(x, mod_lat[3], mod_lat[4])

        idx = layer // 2
        if layer % 2 == 0:
            f = _swiglu(h2, ffn_w_gate[idx], ffn_w_up[idx], ffn_w_down[idx])
        else:
            f = _moe_swiglu(h2, moe_router[idx], moe_w_gate[idx], moe_w_up[idx], moe_w_down[idx])
        x = _layer_norm(DEEPNORM_ALPHA * x + mod_lat[5] * f[:, f.shape[1] - n_lat:], ln2_g[layer], ln2_b[layer])
        if need_ctx:
            ctx = _layer_norm(DEEPNORM_ALPHA * ctx + mod_ctx[5] * f[:, :n_ctx], ln2_g[layer], ln2_b[layer])
    return x
```

```python
import functools
import math

import jax
import jax.numpy as jnp
from jax import lax
from jax.experimental import pallas as pl
from jax.experimental.pallas import tpu as pltpu

F32 = jnp.float32
BF16 = jnp.bfloat16

GRID_W = 64
DA_HEADS = 4
DA_HEAD_DIM = 48
DA_V_DIM = 2 * DA_HEAD_DIM
HY_CH = 256
HY_ORDER = 2
HY_EMB = 33
HY_HIDDEN = 64
HY_DECAY_TARGET = 1e-2
HY_DECAY_SHORT_PCT = 0.3
HY_DECAY_LONG_PCT = 1.5
HY_DECAY_SHIFT = 0.05
MLA_HEADS = 6
MLA_Q_RANK = 256
MLA_KV_RANK = 128
MLA_NOPE = 64
MLA_ROPE = 32
MLA_V = 64
N_EXPERTS = 8
ROPE_BASE = 10000.0
LN_EPS = 1e-5
RMS_EPS = 1e-6

LANES = 128
VMEM_LIMIT = 56 * 1024 * 1024

DA_Q = DA_HEADS * 2 * DA_HEAD_DIM
DA_SLOTS = DA_HEADS * LANES
MLA_SLOTS = MLA_HEADS * LANES
HY_IN = (HY_ORDER + 1) * HY_CH
MIX_EXT = DA_SLOTS + HY_CH + MLA_SLOTS

_C_Q, _C_QR, _C_K, _C_KR, _C_V = 0, 512, 1024, 1536, 2048
_C_HY = 2560
_C_CQ = _C_HY + HY_IN
_C_CKV = _C_CQ + MLA_Q_RANK
_C_KROPE = _C_CKV + MLA_KV_RANK
IN_EXT = _C_KROPE + LANES


def _cparams(sem):
    return pltpu.CompilerParams(dimension_semantics=sem, vmem_limit_bytes=VMEM_LIMIT)


def _resident(shape):
    nd = len(shape)
    return pl.BlockSpec(shape, lambda *_: (0,) * nd, pipeline_mode=pl.Buffered(1))


def _dot(a, b):
    return jnp.dot(a, b, preferred_element_type=F32)


def _dot_nt(a, b):
    return lax.dot_general(a, b, (((1,), (1,)), ((), ())), preferred_element_type=F32)


def _layer_norm_rows(y, g, b):
    mu = jnp.mean(y, axis=-1, keepdims=True)
    yc = y - mu
    var = jnp.mean(yc * yc, axis=-1, keepdims=True)
    return yc * lax.rsqrt(var + LN_EPS) * g + b


def _sigmoid(x):
    return 1.0 / (1.0 + jnp.exp(-x))


def _ada_kernel(c_ref, w_ref, b_ref, o_ref):
    c = c_ref[...]
    act = (c * _sigmoid(c)).astype(BF16)
    o_ref[0] = _dot(act, w_ref[0].astype(BF16)) + b_ref[0]


def _ada_all(c_all, ada_w, ada_b):
    depth, d, n6 = ada_w.shape
    bp = c_all.shape[0]
    tn = 512
    return pl.pallas_call(
        _ada_kernel,
        grid=(depth, n6 // tn),
        in_specs=[
            pl.BlockSpec((bp, d), lambda l, j: (0, 0)),
            pl.BlockSpec((1, d, tn), lambda l, j: (l, 0, j)),
            pl.BlockSpec((1, 1, tn), lambda l, j: (l, 0, j)),
        ],
        out_specs=pl.BlockSpec((1, bp, tn), lambda l, j: (l, 0, j)),
        out_shape=jax.ShapeDtypeStruct((depth, bp, n6), F32),
        compiler_params=_cparams(("parallel", "parallel")),
        name="ada_mod",
    )(c_all, ada_w, ada_b.reshape(depth, 1, n6))


def _tile_lanes(x, reps):
    return jnp.concatenate([x] * reps, axis=-1)


def _rms_rows(x, g):
    return x * lax.rsqrt(jnp.mean(x * x, axis=-1, keepdims=True) + RMS_EPS) * g


def _in_proj_kernel(x_ref, mod_ref, w_ref, tab_ref, qg_ref, kvg_ref, wuq_ref, wuqr_ref, wukv_ref,
                    e2_ref, vbda_ref, vbmla_ref,
                    daq_ref, dak_ref, dav_ref, hy_ref, mlaq_ref, mlak_ref, mlav_ref,
                    *, da_scale, mla_scale):
    m = mod_ref[0]
    h = (x_ref[...] * (1.0 + m[1:2]) + m[0:1]).astype(BF16)

    def seg(a, b):
        return _dot(h, w_ref[:, a:b])

    cos_da = _tile_lanes(tab_ref[:, 0:128], DA_HEADS)
    sin_da = _tile_lanes(tab_ref[:, 128:256], DA_HEADS)
    daq_ref[...] = ((seg(_C_Q, _C_QR) * cos_da + seg(_C_QR, _C_K) * sin_da) * da_scale).astype(BF16)
    dak_ref[...] = (seg(_C_K, _C_KR) * cos_da + seg(_C_KR, _C_V) * sin_da).astype(BF16)
    dav_ref[...] = (seg(_C_V, _C_HY) + vbda_ref[...]).astype(BF16)
    hy_ref[...] = seg(_C_HY, _C_CQ)

    cqn = _rms_rows(seg(_C_CQ, _C_CKV), qg_ref[...]).astype(BF16)
    cos_mq = _tile_lanes(tab_ref[:, 256:384], MLA_HEADS)
    sin_mq = _tile_lanes(tab_ref[:, 384:512], MLA_HEADS)
    mlaq_ref[...] = ((_dot(cqn, wuq_ref[...]) * cos_mq + _dot(cqn, wuqr_ref[...]) * sin_mq)
                     * mla_scale).astype(BF16)

    ckvn = _rms_rows(seg(_C_CKV, _C_KROPE), kvg_ref[...]).astype(BF16)
    kv = _dot(ckvn, wukv_ref[...])
    krw = seg(_C_KROPE, IN_EXT) * tab_ref[:, 512:640]
    kr_hi = krw.astype(BF16)
    kr_lo = (krw - kr_hi.astype(F32)).astype(BF16)
    placed = _dot(kr_hi, e2_ref[...]) + _dot(kr_lo, e2_ref[...])
    mlak_ref[...] = (kv[:, :MLA_SLOTS] + placed).astype(BF16)
    mlav_ref[...] = (kv[:, MLA_SLOTS:] + vbmla_ref[...]).astype(BF16)


def _in_proj(xs, mod, lw, tab, geo):
    t_rows, d = xs.shape
    tm = geo["tm"]
    row = lambda i: (i, 0)
    outs = [(DA_SLOTS, BF16), (DA_SLOTS, BF16), (DA_SLOTS, BF16), (HY_IN, F32),
            (MLA_SLOTS, BF16), (MLA_SLOTS, BF16), (MLA_SLOTS, BF16)]
    kern = functools.partial(_in_proj_kernel, da_scale=DA_HEAD_DIM ** -0.5,
                             mla_scale=(MLA_NOPE + MLA_ROPE) ** -0.5)
    return pl.pallas_call(
        kern,
        grid=(t_rows // tm,),
        in_specs=[
            pl.BlockSpec((tm, d), row),
            pl.BlockSpec((1, 6, d), geo["mod_map"]),
            _resident((d, IN_EXT)),
            pl.BlockSpec((tm, 5 * LANES), geo["tab_map"]),
            _resident((1, MLA_Q_RANK)),
            _resident((1, MLA_KV_RANK)),
            _resident((MLA_Q_RANK, MLA_SLOTS)),
            _resident((MLA_Q_RANK, MLA_SLOTS)),
            _resident((MLA_KV_RANK, 2 * MLA_SLOTS)),
            _resident((LANES, MLA_SLOTS)),
            _resident((1, DA_SLOTS)),
            _resident((1, MLA_SLOTS)),
        ],
        out_specs=[pl.BlockSpec((tm, w), row) for w, _ in outs],
        out_shape=[jax.ShapeDtypeStruct((t_rows, w), dt) for w, dt in outs],
        compiler_params=_cparams(("parallel",)),
        name="in_proj",
    )(xs, mod, lw["w_in"], tab, lw["qg"], lw["kvg"], lw["wuq"], lw["wuqr"], lw["wukv"],
      lw["e2"], lw["vb_da"], lw["vb_mla"])


def _attn_kernel(*refs, n_maps, dv, n_lat_tiles, post_scale, has_ctx_q):
    if n_maps == 2:
        lam_ref, g_ref, q_ref, kl_ref, kc_ref, vl_ref, vc_ref, o_ref = refs
    else:
        q_ref, kl_ref, kc_ref, vl_ref, vc_ref, o_ref = refs
    q = q_ref[...]
    lane = lax.broadcasted_iota(jnp.int32, q.shape, 1)

    def attend(qm, include_lat):
        kc = kc_ref[...]
        sc = _dot_nt(qm, kc)
        mx = jnp.max(sc, axis=-1, keepdims=True)
        if include_lat:
            sl = _dot_nt(qm, kl_ref[...])
            mx = jnp.maximum(mx, jnp.max(sl, axis=-1, keepdims=True))
            ol = _dot(jnp.exp(sl - mx).astype(BF16), vl_ref[...])
        ol_c = _dot(jnp.exp(sc - mx).astype(BF16), vc_ref[...])
        ol = ol + ol_c if include_lat else ol_c
        den = jnp.sum(jnp.where(lane == dv, ol, 0.0), axis=-1, keepdims=True)
        return ol / den

    def run(include_lat):
        if n_maps == 2:
            q1 = jnp.where(lane < DA_HEAD_DIM, q, jnp.zeros_like(q))
            q2 = jnp.where(lane < DA_HEAD_DIM, jnp.zeros_like(q), q)
            lp = lam_ref[...]
            lam = (jnp.exp(jnp.sum(lp[0:1] * lp[1:2], axis=-1, keepdims=True))
                   - jnp.exp(jnp.sum(lp[2:3] * lp[3:4], axis=-1, keepdims=True))
                   + (1.0 - post_scale))
            o = attend(q1, include_lat) - lam * attend(q2, include_lat)
            o = jnp.where(lane < dv, o, 0.0)
            ms = jnp.sum(o * o, axis=-1, keepdims=True) * (1.0 / dv)
            o = o * lax.rsqrt(ms + RMS_EPS) * g_ref[...] * post_scale
        else:
            o = jnp.where(lane < dv, attend(q, include_lat), 0.0)
        o_ref[...] = o.astype(o_ref.dtype)

    if has_ctx_q:
        j = pl.program_id(2)

        @pl.when(j < n_lat_tiles)
        def _():
            run(True)

        @pl.when(j >= n_lat_tiles)
        def _():
            run(False)
    else:
        run(True)


def _attention(q, k, v, geo, *, heads, n_maps, dv, need_ctx, post_scale=1.0, lam=None, gain=None):
    b, nl, nc, tq = geo["b"], geo["nl"], geo["nc"], geo["tq"]
    n_lat_tiles = nl // tq
    n_q_tiles = n_lat_tiles + (nc // tq if need_ctx else 0)
    lat_blocks = b * nl // tq
    ctx_kv0 = b * nl // nc

    def q_map(bi, hi, j):
        r = jnp.where(j < n_lat_tiles, bi * n_lat_tiles + j,
                      lat_blocks + bi * (nc // tq) + (j - n_lat_tiles))
        return (r, hi)

    kv_lat = pl.BlockSpec((nl, LANES), lambda bi, hi, j: (bi, hi))
    kv_ctx = pl.BlockSpec((nc, LANES), lambda bi, hi, j: (ctx_kv0 + bi, hi))
    in_specs = [pl.BlockSpec((tq, LANES), q_map), kv_lat, kv_ctx, kv_lat, kv_ctx]
    args = [q, k, k, v, v]
    if n_maps == 2:
        in_specs = [pl.BlockSpec((4, LANES), lambda bi, hi, j: (0, 0)),
                    pl.BlockSpec((1, LANES), lambda bi, hi, j: (0, 0))] + in_specs
        args = [lam, gain] + args
    kern = functools.partial(_attn_kernel, n_maps=n_maps, dv=dv, n_lat_tiles=n_lat_tiles,
                             post_scale=post_scale, has_ctx_q=need_ctx)
    return pl.pallas_call(
        kern,
        grid=(b, heads, n_q_tiles),
        in_specs=in_specs,
        out_specs=pl.BlockSpec((tq, LANES), q_map),
        out_shape=jax.ShapeDtypeStruct((b * n_q_tiles * tq, q.shape[1]), BF16),
        compiler_params=_cparams(("parallel", "parallel", "arbitrary")),
        name="diff_attn" if n_maps == 2 else "mla_attn",
    )(*args)


def _hy_filter_kernel(z_ref, w1_ref, b1_ref, w2_ref, b2_ref, w3_ref, b3_ref, w4_ref, fr_ref,
                      win_ref, sgn_ref, a_ref, d_ref, hn_ref, *, rows):
    hp = lax.Precision.HIGHEST
    fr = fr_ref[...]

    def lin(x, w_ref, b_ref):
        return jnp.dot(x, w_ref[...], precision=hp, preferred_element_type=F32) + b_ref[...]

    hdn = jnp.sin(fr * lin(z_ref[...], w1_ref, b1_ref))
    hdn = jnp.sin(fr * lin(hdn, w2_ref, b2_ref))
    hdn = jnp.sin(fr * lin(hdn, w3_ref, b3_ref))
    filt = jnp.dot(hdn, w4_ref[...], precision=hp, preferred_element_type=F32)
    win = _tile_lanes(win_ref[...], HY_ORDER)
    half = HY_ORDER * HY_CH
    fwd = filt[:, :half] * win
    bwd = filt[:, half:] * win
    i = pl.program_id(0)
    row = lax.broadcasted_iota(jnp.int32, bwd.shape, 0) + i * rows
    bwd = jnp.where(row == 0, 0.0, bwd)
    a = fwd + bwd
    a_ref[...] = a.astype(BF16)
    d_ref[...] = (bwd - fwd).astype(BF16)

    @pl.when(i == 0)
    def _():
        hn_ref[...] = jnp.zeros_like(hn_ref)

    hn_ref[...] += jnp.sum(a * sgn_ref[...], axis=0, keepdims=True)


def _hy_spectrum_kernel(c_ref, s_ref, a_ref, d_ref, hr_ref, hi_ref):
    hr_ref[...] = _dot(c_ref[...], a_ref[...])
    hi_ref[...] = _dot(s_ref[...], d_ref[...])


def _hy_filter(hw, consts):
    n = consts["c"].shape[0]
    half = HY_ORDER * HY_CH
    tr = _pick_tile(512, n)
    row = lambda i: (i, 0)
    weights = (hw["fw1"], hw["fb1"], hw["fw2"], hw["fb2"], hw["fw3"], hw["fb3"], hw["fw4"], hw["freq"])
    a, d, hn = pl.pallas_call(
        functools.partial(_hy_filter_kernel, rows=tr),
        grid=(n // tr,),
        in_specs=([pl.BlockSpec((tr, HY_HIDDEN), row)] + [_resident(w.shape) for w in weights]
                  + [pl.BlockSpec((tr, HY_CH), row), pl.BlockSpec((tr, 1), row)]),
        out_specs=[pl.BlockSpec((tr, half), row), pl.BlockSpec((tr, half), row),
                   pl.BlockSpec((1, half), lambda i: (0, 0))],
        out_shape=[jax.ShapeDtypeStruct((n, half), BF16), jax.ShapeDtypeStruct((n, half), BF16),
                   jax.ShapeDtypeStruct((1, half), F32)],
        compiler_params=_cparams(("arbitrary",)),
        name="hyena_filter",
    )(consts["z"], *weights, consts["win"], consts["sgn"])
    col = lambda j: (0, j)
    hr, hi = pl.pallas_call(
        _hy_spectrum_kernel,
        grid=(half // HY_CH,),
        in_specs=[_resident((n, n)), _resident((n, n)), pl.BlockSpec((n, HY_CH), col),
                  pl.BlockSpec((n, HY_CH), col)],
        out_specs=[pl.BlockSpec((n, HY_CH), col), pl.BlockSpec((n, HY_CH), col)],
        out_shape=[jax.ShapeDtypeStruct((n, half), F32), jax.ShapeDtypeStruct((n, half), F32)],
        compiler_params=_cparams(("parallel",)),
        name="hyena_spectrum",
    )(consts["c"], consts["s"], a, d)
    return hr, hi, hn


def _hy_conv_kernel(*refs, n, kc, rc):
    (p_ref, cw_ref, cb_ref, c_ref, s_ref, hr_ref, hi_ref, hn_ref, sgn_ref, bias_ref, o_ref,
     z_s, zb_s, y_s) = refs
    ri = lax.broadcasted_iota(jnp.int32, (rc, HY_CH), 0)
    zero_row = jnp.zeros((1, HY_CH), F32)

    def short_conv(g, r0):
        cs = slice(g * HY_CH, (g + 1) * HY_CH)
        p = p_ref[r0:r0 + rc, cs]
        up = p_ref[r0 - 1:r0, cs] if r0 > 0 else zero_row
        dn = p_ref[r0 + rc:r0 + rc + 1, cs] if r0 + rc < n else zero_row
        prev = jnp.where(ri == 0, up, pltpu.roll(p, 1, 0))
        nxt = jnp.where(ri == rc - 1, dn, pltpu.roll(p, rc - 1, 0))
        return prev * cw_ref[0, :, cs] + p * cw_ref[1, :, cs] + nxt * cw_ref[2, :, cs] + cb_ref[:, cs]

    for r0 in range(0, n, rc):
        z_s[r0:r0 + rc, :] = short_conv(HY_ORDER, r0)
    for i in range(HY_ORDER):
        cs = slice(i * HY_CH, (i + 1) * HY_CH)
        zb_s[...] = z_s[...].astype(BF16)
        xn = jnp.sum(z_s[...] * sgn_ref[...], axis=0, keepdims=True)
        y_s[...] = sgn_ref[...] * (xn * hn_ref[:, cs] * (0.5 / n))
        for k0 in range(0, n, kc):
            ks = slice(k0, k0 + kc)
            zb = zb_s[...]
            a = _dot(c_ref[ks, :], zb)
            bm = _dot(s_ref[ks, :], zb)
            hr = hr_ref[ks, cs]
            hi = hi_ref[ks, cs]
            yr = a * hr + bm * hi
            if k0 == 0:
                yr = jnp.where(lax.broadcasted_iota(jnp.int32, yr.shape, 0) == 0, 0.5 * yr, yr)
            yi = a * hi - bm * hr
            y_s[...] += (_dot(c_ref[:, ks], yr.astype(BF16)) - _dot(s_ref[:, ks], yi.astype(BF16))) * (1.0 / n)
        dst = o_ref if i == HY_ORDER - 1 else z_s
        for r0 in range(0, n, rc):
            rs = slice(r0, r0 + rc)
            dst[rs, :] = short_conv(i, r0) * (y_s[rs, :] + z_s[rs, :] * bias_ref[i])


def _hy_conv(hy_p, hw, spec, consts, geo, *, ctx):
    b, nl, nc = geo["b"], geo["nl"], geo["nc"]
    n = nc if ctx else nl
    blk0 = b * nl // nc if ctx else 0
    hr, hi, hn = spec
    args = [hy_p, hw["conv_w"], hw["conv_b"], consts["c"], consts["s"], hr, hi, hn, consts["sgn"],
            hw["bias"]]
    in_specs = ([pl.BlockSpec((n, HY_IN), lambda bi: (blk0 + bi, 0), pipeline_mode=pl.Buffered(1))]
                + [_resident(a.shape) for a in args[1:]])
    kern = functools.partial(_hy_conv_kernel, n=n, kc=_pick_tile(512, n), rc=_pick_tile(256, n))
    return pl.pallas_call(
        kern,
        grid=(b,),
        in_specs=in_specs,
        out_specs=pl.BlockSpec((n, HY_CH), lambda bi: (bi, 0)),
        out_shape=jax.ShapeDtypeStruct((b * n, HY_CH), F32),
        scratch_shapes=[pltpu.VMEM((n, HY_CH), F32), pltpu.VMEM((n, HY_CH), BF16), pltpu.VMEM((n, HY_CH), F32)],
        compiler_params=_cparams(("parallel",)),
        name="hyena_conv_ctx" if ctx else "hyena_conv_lat",
    )(*args)


def _top2_route(logits):
    lane = lax.broadcasted_iota(jnp.int32, logits.shape, 1)
    neg = jnp.float32(-jnp.inf)
    lg = jnp.where(lane < N_EXPERTS, logits, neg)
    m1 = jnp.max(lg, axis=-1, keepdims=True)
    i1 = jnp.min(jnp.where(lg == m1, lane, LANES), axis=-1, keepdims=True)
    lg2 = jnp.where(lane == i1, neg, lg)
    m2 = jnp.max(lg2, axis=-1, keepdims=True)
    i2 = jnp.min(jnp.where(lg2 == m2, lane, LANES), axis=-1, keepdims=True)
    e = jnp.exp(m2 - m1)
    g1 = 1.0 / (1.0 + e)
    g2 = e * g1
    out = jnp.where(lane == 0, i1.astype(F32), 0.0)
    out = jnp.where(lane == 1, i2.astype(F32), out)
    out = jnp.where(lane == 2, g1, out)
    return jnp.where(lane == 3, g2, out)


def _out_proj_kernel(*refs, alpha, moe):
    if moe:
        (da_ref, hy_ref, mla_ref, w_ref, x_ref, mod_ref, g_ref, b_ref, rw_ref,
         x1_ref, h2_ref, route_ref) = refs
    else:
        da_ref, hy_ref, mla_ref, w_ref, x_ref, mod_ref, g_ref, b_ref, x1_ref, h2_ref = refs
    m = mod_ref[0]
    o = (_dot(da_ref[...], w_ref[0:DA_SLOTS])
         + _dot(hy_ref[...].astype(BF16), w_ref[DA_SLOTS:DA_SLOTS + HY_CH])
         + _dot(mla_ref[...], w_ref[DA_SLOTS + HY_CH:MIX_EXT]))
    x1 = _layer_norm_rows(alpha * x_ref[...] + m[2:3] * o, g_ref[...], b_ref[...])
    x1_ref[...] = x1
    h2 = x1 * (1.0 + m[4:5]) + m[3:4]
    h2_ref[...] = h2.astype(h2_ref.dtype)
    if moe:
        route_ref[...] = _top2_route(_dot(h2.astype(BF16), rw_ref[...]))


def _out_proj(da_o, hy_o, mla_o, xs, mod, lw, geo, *, t_rows, alpha, router=None):
    d = xs.shape[1]
    tm = geo["tm"]
    row = lambda i: (i, 0)
    moe = router is not None
    in_specs = [pl.BlockSpec((tm, DA_SLOTS), row), pl.BlockSpec((tm, HY_CH), row),
                pl.BlockSpec((tm, MLA_SLOTS), row), _resident((MIX_EXT, d)), pl.BlockSpec((tm, d), row),
                pl.BlockSpec((1, 6, d), geo["mod_map"]), _resident((1, d)), _resident((1, d))]
    args = [da_o, hy_o, mla_o, lw["w_out"], xs, mod, lw["ln1_g"], lw["ln1_b"]]
    out_specs = [pl.BlockSpec((tm, d), row), pl.BlockSpec((tm, d), row)]
    out_shape = [jax.ShapeDtypeStruct((t_rows, d), F32),
                 jax.ShapeDtypeStruct((t_rows, d), F32 if moe else BF16)]
    if moe:
        in_specs.append(_resident((d, LANES)))
        args.append(router)
        out_specs.append(pl.BlockSpec((tm, LANES), row))
        out_shape.append(jax.ShapeDtypeStruct((t_rows, LANES), F32))
    return pl.pallas_call(
        functools.partial(_out_proj_kernel, alpha=alpha, moe=moe),
        grid=(t_rows // tm,),
        in_specs=in_specs,
        out_specs=out_specs,
        out_shape=out_shape,
        compiler_params=_cparams(("parallel",)),
        name="out_proj_ln1",
    )(*args)


def _swiglu_kernel(*refs, alpha, grouped):
    if grouped:
        be_ref, na_ref, h_ref, wg_ref, wu_ref, wd_ref, o_ref, acc_ref = refs
    else:
        h_ref, wg_ref, wu_ref, wd_ref, x_ref, mod_ref, g_ref, b_ref, o_ref, acc_ref = refs
    f = pl.program_id(1)
    nf = pl.num_programs(1)

    def step():
        @pl.when(f == 0)
        def _():
            acc_ref[...] = jnp.zeros_like(acc_ref)

        h = h_ref[...].astype(BF16)
        g = _dot(h, wg_ref[0])
        u = _dot(h, wu_ref[0])
        act = (g * _sigmoid(g) * u).astype(BF16)
        acc_ref[...] += _dot(act, wd_ref[0])

    if grouped:
        active = pl.program_id(0) < na_ref[0]
        pl.when(active)(step)

        @pl.when(f == nf - 1)
        def _():
            o_ref[...] = jnp.where(active, acc_ref[...], 0.0)
    else:
        step()

        @pl.when(f == nf - 1)
        def _():
            m = mod_ref[0]
            o_ref[...] = _layer_norm_rows(alpha * x_ref[...] + m[5:6] * acc_ref[...],
                                          g_ref[...], b_ref[...])


def _ffn_dense(h2, wg, wu, wd, x1, mod, lw, geo, *, alpha):
    t_rows, d = h2.shape
    f_dim = wg.shape[-1]
    tm, tf = geo["tm_ffn"], geo["tf"]
    row = lambda i, f: (i, 0)
    mod_map = geo["mod_map_ffn"]
    return pl.pallas_call(
        functools.partial(_swiglu_kernel, alpha=alpha, grouped=False),
        grid=(t_rows // tm, f_dim // tf),
        in_specs=[
            pl.BlockSpec((tm, d), row),
            pl.BlockSpec((1, d, tf), lambda i, f: (0, 0, f)),
            pl.BlockSpec((1, d, tf), lambda i, f: (0, 0, f)),
            pl.BlockSpec((1, tf, d), lambda i, f: (0, f, 0)),
            pl.BlockSpec((tm, d), row),
            pl.BlockSpec((1, 6, d), lambda i, f: mod_map(i)),
            pl.BlockSpec((1, d), lambda i, f: (0, 0)),
            pl.BlockSpec((1, d), lambda i, f: (0, 0)),
        ],
        out_specs=pl.BlockSpec((tm, d), row),
        out_shape=jax.ShapeDtypeStruct((t_rows, d), F32),
        scratch_shapes=[pltpu.VMEM((tm, d), F32)],
        compiler_params=_cparams(("parallel", "arbitrary")),
        name="ffn_swiglu_ln2",
    )(h2, wg, wu, wd, x1, mod, lw["ln2_g"], lw["ln2_b"])


def _ffn_grouped(rows, wg, wu, wd, block_expert, n_active, geo):
    r_rows, d = rows.shape
    f_dim = wg.shape[-1]
    mb, tf = geo["mb"], geo["tf"]
    nf = f_dim // tf

    def f_idx(i, f, na):
        return jnp.where(i < na[0], f, nf - 1)

    grid_spec = pltpu.PrefetchScalarGridSpec(
        num_scalar_prefetch=2,
        grid=(r_rows // mb, nf),
        in_specs=[
            pl.BlockSpec((mb, d), lambda i, f, be, na: (jnp.minimum(i, na[0] - 1), 0)),
            pl.BlockSpec((1, d, tf), lambda i, f, be, na: (be[i], 0, f_idx(i, f, na))),
            pl.BlockSpec((1, d, tf), lambda i, f, be, na: (be[i], 0, f_idx(i, f, na))),
            pl.BlockSpec((1, tf, d), lambda i, f, be, na: (be[i], f_idx(i, f, na), 0)),
        ],
        out_specs=pl.BlockSpec((mb, d), lambda i, f, be, na: (i, 0)),
        scratch_shapes=[pltpu.VMEM((mb, d), F32)],
    )
    return pl.pallas_call(
        functools.partial(_swiglu_kernel, alpha=0.0, grouped=True),
        grid_spec=grid_spec,
        out_shape=jax.ShapeDtypeStruct((r_rows, d), F32),
        compiler_params=_cparams(("arbitrary", "arbitrary")),
        name="moe_experts",
    )(block_expert, n_active, rows, wg, wu, wd)


def _gather_kernel(idx_ref, src_ref, o_ref, sem, *, rows):
    def copy(r, src_row):
        return pltpu.make_async_copy(src_ref.at[pl.ds(src_row, 1)], o_ref.at[pl.ds(r, 1)], sem)

    def start(r, carry):
        copy(r, idx_ref[r]).start()
        return carry

    def wait(r, carry):
        copy(r, 0).wait()
        return carry

    lax.fori_loop(0, rows, start, 0)
    lax.fori_loop(0, rows, wait, 0)


def _gather_rows(src, idx, rows_per_step):
    m = idx.shape[0]
    d = src.shape[1]
    return pl.pallas_call(
        functools.partial(_gather_kernel, rows=rows_per_step),
        grid=(m // rows_per_step,),
        in_specs=[pl.BlockSpec((rows_per_step,), lambda i: (i,), memory_space=pltpu.SMEM),
                  pl.BlockSpec(memory_space=pl.ANY)],
        out_specs=pl.BlockSpec((rows_per_step, d), lambda i: (i, 0)),
        out_shape=jax.ShapeDtypeStruct((m, d), src.dtype),
        scratch_shapes=[pltpu.SemaphoreType.DMA(())],
        compiler_params=_cparams(("arbitrary",)),
        name="gather_rows",
    )(idx, src)


def _moe_ln2_kernel(x_ref, y1_ref, y2_ref, route_ref, mod_ref, g_ref, b_ref, o_ref, *, alpha):
    m = mod_ref[0]
    rt = route_ref[...]
    lane = lax.broadcasted_iota(jnp.int32, rt.shape, 1)
    g1 = jnp.sum(jnp.where(lane == 2, rt, 0.0), axis=-1, keepdims=True)
    g2 = jnp.sum(jnp.where(lane == 3, rt, 0.0), axis=-1, keepdims=True)
    f = y1_ref[...] * g1 + y2_ref[...] * g2
    o_ref[...] = _layer_norm_rows(alpha * x_ref[...] + m[5:6] * f, g_ref[...], b_ref[...])


def _moe_ln2(x1, pairs, route, mod, lw, geo, *, alpha):
    t_rows, d = x1.shape
    tm = geo["tm"]
    nt = t_rows // tm
    row = lambda i: (i, 0)
    return pl.pallas_call(
        functools.partial(_moe_ln2_kernel, alpha=alpha),
        grid=(nt,),
        in_specs=[pl.BlockSpec((tm, d), row), pl.BlockSpec((tm, d), row),
                  pl.BlockSpec((tm, d), lambda i: (nt + i, 0)), pl.BlockSpec((tm, LANES), row),
                  pl.BlockSpec((1, 6, d), geo["mod_map"]), _resident((1, d)), _resident((1, d))],
        out_specs=pl.BlockSpec((tm, d), row),
        out_shape=jax.ShapeDtypeStruct((t_rows, d), F32),
        compiler_params=_cparams(("parallel",)),
        name="moe_combine_ln2",
    )(x1, pairs, pairs, route, mod, lw["ln2_g"], lw["ln2_b"])


def _moe_layer(x1, h2, route, mod, lw, geo, *, alpha):
    t_rows = x1.shape[0]
    mb = geo["mb"]
    n_assign = 2 * t_rows
    n_blocks = -(-(n_assign + N_EXPERTS * (mb - 1)) // mb)
    expert_of = route[:, 0:2].astype(jnp.int32).reshape(n_assign)
    onehot = (expert_of[:, None] == jnp.arange(N_EXPERTS, dtype=jnp.int32)[None, :]).astype(jnp.int32)
    csum = jnp.cumsum(onehot, axis=0)
    rank = jnp.take_along_axis(csum, expert_of[:, None], axis=1)[:, 0] - 1
    counts = csum[-1]
    padded = (counts + mb - 1) // mb * mb
    padded_end = jnp.cumsum(padded)
    dest = (padded_end - padded)[expert_of] + rank
    n_active = (padded_end[-1] // mb).astype(jnp.int32).reshape(1)
    block_expert = jnp.minimum(
        jnp.searchsorted(padded_end, jnp.arange(n_blocks, dtype=jnp.int32) * mb, side="right"),
        N_EXPERTS - 1).astype(jnp.int32)
    token_of_row = jnp.zeros((n_blocks * mb,), jnp.int32).at[dest].set(
        jnp.arange(n_assign, dtype=jnp.int32) // 2)
    rows = _gather_rows(h2, token_of_row, mb)
    out = _ffn_grouped(rows, lw["moe_wg"], lw["moe_wu"], lw["moe_wd"], block_expert, n_active, geo)
    pair_idx = jnp.concatenate([dest[0::2], dest[1::2]]).astype(jnp.int32)
    pairs = _gather_rows(out, pair_idx, geo["tm"])
    return _moe_ln2(x1, pairs, route, mod, lw, geo, alpha=alpha)


def _rope_tables(n_lat, tm):
    pos = jnp.arange(n_lat, dtype=jnp.int32)
    row = (pos // GRID_W).astype(F32)
    col = (pos % GRID_W).astype(F32)

    def axial(rot_dim):
        axis_dim = rot_dim // 2
        inv = ROPE_BASE ** (-jnp.arange(0, axis_dim, 2, dtype=F32) / axis_dim)
        ang_r = row[:, None] * inv[None, :]
        ang_c = col[:, None] * inv[None, :]
        ang = jnp.concatenate([ang_r, ang_r, ang_c, ang_c], axis=-1)
        return jnp.cos(ang), jnp.sin(ang)

    def pad(x, width, fill=0.0):
        return jnp.pad(x, ((0, 0), (0, width - x.shape[1])), constant_values=fill)

    cd, sd = axial(DA_HEAD_DIM)
    cm, sm = axial(MLA_ROPE)
    ones_nope = jnp.ones((n_lat, MLA_NOPE), F32)
    zeros_nope = jnp.zeros((n_lat, MLA_NOPE), F32)
    lat = jnp.concatenate([
        pad(jnp.concatenate([cd, cd], -1), LANES), pad(jnp.concatenate([sd, sd], -1), LANES),
        pad(jnp.concatenate([ones_nope, cm], -1), LANES), pad(jnp.concatenate([zeros_nope, sm], -1), LANES),
        pad(jnp.concatenate([cm, sm], -1), LANES)], axis=-1)
    ident_row = jnp.concatenate([
        pad(jnp.ones((1, 2 * DA_HEAD_DIM), F32), LANES), jnp.zeros((1, LANES), F32),
        pad(jnp.ones((1, MLA_NOPE + MLA_ROPE), F32), LANES), jnp.zeros((1, LANES), F32),
        pad(jnp.ones((1, MLA_ROPE), F32), LANES)], axis=-1)
    return jnp.concatenate([lat, jnp.broadcast_to(ident_row, (tm, 5 * LANES))], axis=0)


def _rot_cols(w, dim):
    lead = w.shape[:-1]
    g = w.reshape(lead + (-1, 2, dim // 2))
    return jnp.concatenate([-g[..., 1:2, :], g[..., 0:1, :]], axis=-2).reshape(w.shape)


def _to_slots(w, heads, width):
    lead = w.shape[:-1]
    g = w.reshape(lead + (heads, width))
    g = jnp.pad(g, [(0, 0)] * len(lead) + [(0, 0), (0, LANES - width)])
    return g.reshape(lead + (heads * LANES,))


def _layer_weights(l, p):
    d = p["w_in"].shape[1]
    w_in = p["w_in"][l]
    da = w_in[:, :3 * DA_Q]
    wq, wk, wv = da[:, :DA_Q], da[:, DA_Q:2 * DA_Q], da[:, 2 * DA_Q:]
    half = DA_HEAD_DIM // 2
    hy = w_in[:, 3 * DA_Q:3 * DA_Q + HY_IN]
    mla = w_in[:, 3 * DA_Q + HY_IN:]
    w_cq = mla[:, :MLA_Q_RANK]
    w_ckv = mla[:, MLA_Q_RANK:MLA_Q_RANK + MLA_KV_RANK]
    w_kr = mla[:, MLA_Q_RANK + MLA_KV_RANK:]
    kr_seg = jnp.pad(jnp.concatenate([w_kr, _rot_cols(w_kr, MLA_ROPE // 2)], -1),
                     ((0, 0), (0, LANES - 2 * MLA_ROPE)))
    w_in_ext = jnp.concatenate([
        _to_slots(wq, DA_HEADS, DA_V_DIM), _to_slots(_rot_cols(wq, half), DA_HEADS, DA_V_DIM),
        _to_slots(wk, DA_HEADS, DA_V_DIM), _to_slots(_rot_cols(wk, half), DA_HEADS, DA_V_DIM),
        _to_slots(wv, DA_HEADS, DA_V_DIM), hy, w_cq, w_ckv, kr_seg], axis=-1).astype(BF16)

    qh = MLA_NOPE + MLA_ROPE
    wuq = p["mla_w_uq"][l].reshape(MLA_Q_RANK, MLA_HEADS, qh)
    wuq_rot = jnp.concatenate([jnp.zeros_like(wuq[..., :MLA_NOPE]),
                               _rot_cols(wuq[..., MLA_NOPE:], MLA_ROPE // 2)], -1)
    wukv = p["mla_w_ukv"][l].reshape(MLA_KV_RANK, MLA_HEADS, MLA_NOPE + MLA_V)
    wuk = _to_slots(wukv[..., :MLA_NOPE].reshape(MLA_KV_RANK, -1), MLA_HEADS, MLA_NOPE)
    wuv = _to_slots(wukv[..., MLA_NOPE:].reshape(MLA_KV_RANK, -1), MLA_HEADS, MLA_V)
    lane_src = jnp.arange(LANES)[:, None]
    lane_dst = jnp.arange(MLA_SLOTS)[None, :] % LANES
    e2 = ((lane_src < 2 * MLA_ROPE) & (lane_dst == MLA_NOPE + lane_src % MLA_ROPE)).astype(BF16)

    slot_lane = jnp.arange(DA_SLOTS) % LANES
    vb_da = (slot_lane == DA_V_DIM).astype(F32)[None, :]
    vb_mla = ((jnp.arange(MLA_SLOTS) % LANES) == MLA_V).astype(F32)[None, :]

    w_out = p["w_out"][l]
    w_out_ext = jnp.concatenate([
        _to_slots(w_out[:DA_HEADS * DA_V_DIM].T, DA_HEADS, DA_V_DIM).T,
        w_out[DA_HEADS * DA_V_DIM:DA_HEADS * DA_V_DIM + HY_CH],
        _to_slots(w_out[DA_HEADS * DA_V_DIM + HY_CH:].T, MLA_HEADS, MLA_V).T], axis=0).astype(BF16)

    def pad_lanes(v):
        return jnp.pad(v, (0, LANES - v.shape[0]))[None, :]

    lam = jnp.concatenate([pad_lanes(p[k][l]) for k in
                           ("da_lambda_q1", "da_lambda_k1", "da_lambda_q2", "da_lambda_k2")], axis=0)
    hw = {
        "conv_w": p["hy_conv_w"][l][:, None, :], "conv_b": p["hy_conv_b"][l][None, :],
        "fw1": jnp.pad(p["hy_fw1"][l], ((0, HY_HIDDEN - HY_EMB), (0, 0))), "fb1": p["hy_fb1"][l][None, :],
        "fw2": p["hy_fw2"][l], "fb2": p["hy_fb2"][l][None, :],
        "fw3": p["hy_fw3"][l], "fb3": p["hy_fb3"][l][None, :],
        "fw4": p["hy_fw4"][l], "freq": p["hy_freq"][l][None, :], "bias": p["hy_bias"][l][:, None, :],
    }
    return {
        "w_in": w_in_ext, "qg": p["mla_q_norm_g"][l][None, :], "kvg": p["mla_kv_norm_g"][l][None, :],
        "wuq": _to_slots(wuq.reshape(MLA_Q_RANK, -1), MLA_HEADS, qh).astype(BF16),
        "wuqr": _to_slots(wuq_rot.reshape(MLA_Q_RANK, -1), MLA_HEADS, qh).astype(BF16),
        "wukv": jnp.concatenate([wuk, wuv], -1).astype(BF16), "e2": e2,
        "vb_da": vb_da, "vb_mla": vb_mla, "w_out": w_out_ext, "lam": lam,
        "subln_g": pad_lanes(p["da_subln_g"][l]), "hy": hw,
        "ln1_g": p["ln1_g"][l][None, :], "ln1_b": p["ln1_b"][l][None, :],
        "ln2_g": p["ln2_g"][l][None, :], "ln2_b": p["ln2_b"][l][None, :],
    }


def _hyena_consts(n):
    t = jnp.linspace(0.0, 1.0, n, dtype=F32)[:, None]
    bands = (HY_EMB - 1) // 2
    w = 2.0 * math.pi * jnp.arange(n, dtype=F32)[:, None] / n
    f = jnp.linspace(1e-4, bands - 1, bands, dtype=F32)[None, :]
    z = jnp.concatenate([t, jnp.cos(f * w), -jnp.sin(f * w)], axis=-1)
    z = jnp.pad(z, ((0, 0), (0, HY_HIDDEN - HY_EMB)))
    max_decay = math.log(HY_DECAY_TARGET) / HY_DECAY_SHORT_PCT
    min_decay = math.log(HY_DECAY_TARGET) / HY_DECAY_LONG_PCT
    deltas = jnp.linspace(min_decay, max_decay, HY_CH, dtype=F32)
    win = jnp.exp(-t * jnp.abs(deltas)[None, :]) + HY_DECAY_SHIFT
    k = jnp.arange(n, dtype=jnp.int32)
    ang = ((k[:, None] * k[None, :]) % (2 * n)).astype(F32) * (math.pi / n)
    sgn = (1 - 2 * (k % 2)).astype(F32)[:, None]
    return {"z": z, "win": win, "sgn": sgn, "c": jnp.cos(ang).astype(BF16), "s": jnp.sin(ang).astype(BF16)}


def _pick_tile(limit, *sizes):
    t = limit
    while any(s % t for s in sizes):
        t //= 2
    return t


def _make_geo(b, nl, nc):
    t_lat = b * nl
    tm = _pick_tile(512, nl, b * nc)
    tm_ffn = _pick_tile(1024, nl, b * nc)
    geo = {"b": b, "nl": nl, "nc": nc, "tm": tm, "tm_ffn": tm_ffn, "tf": 256,
           "tq": _pick_tile(256, nl, nc), "mb": 512}
    n_lat_t, per_b = t_lat // tm, nl // tm
    geo["mod_map"] = lambda i: (jnp.where(i < n_lat_t, i // per_b, b), 0, 0)
    geo["tab_map"] = lambda i: (jnp.where(i < n_lat_t, i % per_b, per_b), 0)
    n_lat_f, per_b_f = t_lat // tm_ffn, nl // tm_ffn
    geo["mod_map_ffn"] = lambda i: (jnp.where(i < n_lat_f, i // per_b_f, b), 0, 0)
    return geo


def kernel(x, c, ctx, c_ctx, ada_w, ada_b, w_in, da_lambda_q1, da_lambda_k1, da_lambda_q2, da_lambda_k2, da_subln_g, hy_conv_w, hy_conv_b, hy_fw1, hy_fb1, hy_fw2, hy_fb2, hy_fw3, hy_fb3, hy_fw4, hy_freq, hy_bias, mla_q_norm_g, mla_w_uq, mla_kv_norm_g, mla_w_ukv, w_out, ln1_g, ln1_b, ln2_g, ln2_b, ffn_w_gate, ffn_w_up, ffn_w_down, moe_router, moe_w_gate, moe_w_up, moe_w_down):
    p = dict(w_in=w_in, da_lambda_q1=da_lambda_q1, da_lambda_k1=da_lambda_k1, da_lambda_q2=da_lambda_q2,
             da_lambda_k2=da_lambda_k2, da_subln_g=da_subln_g, hy_conv_w=hy_conv_w, hy_conv_b=hy_conv_b,
             hy_fw1=hy_fw1, hy_fb1=hy_fb1, hy_fw2=hy_fw2, hy_fb2=hy_fb2, hy_fw3=hy_fw3, hy_fb3=hy_fb3,
             hy_fw4=hy_fw4, hy_freq=hy_freq, hy_bias=hy_bias, mla_q_norm_g=mla_q_norm_g, mla_w_uq=mla_w_uq,
             mla_kv_norm_g=mla_kv_norm_g, mla_w_ukv=mla_w_ukv, w_out=w_out, ln1_g=ln1_g, ln1_b=ln1_b,
             ln2_g=ln2_g, ln2_b=ln2_b)
    b, nl, d = x.shape
    nc = ctx.shape[1]
    depth = ada_w.shape[0]
    assert nl % nc == 0 and nl % GRID_W == 0 and nc % 8 == 0
    alpha = (2 * depth) ** 0.25
    t_lat, t_all = b * nl, b * (nl + nc)

    geo = _make_geo(b, nl, nc)
    tm = geo["tm"]

    bp = -(-(b + 1) // 8) * 8
    c_all = jnp.zeros((bp, d), F32).at[:b].set(c).at[b].set(c_ctx)
    mod_all = _ada_all(c_all, ada_w, ada_b).reshape(depth, bp, 6, d)

    tab = _rope_tables(nl, tm)
    consts_lat = _hyena_consts(nl)
    consts_ctx = _hyena_consts(nc)
    xs = jnp.concatenate([x.reshape(t_lat, d), ctx.reshape(b * nc, d)], axis=0)

    for l in range(depth):
        need_ctx = l < depth - 1
        t_out = t_all if need_ctx else t_lat
        lw = _layer_weights(l, p)
        mod = mod_all[l]
        lambda_init = 0.8 - 0.6 * math.exp(-0.3 * l)

        daq, dak, dav, hy_p, mlaq, mlak, mlav = _in_proj(xs, mod, lw, tab, geo)
        da_o = _attention(daq, dak, dav, geo, heads=DA_HEADS, n_maps=2, dv=DA_V_DIM, need_ctx=need_ctx,
                          post_scale=1.0 - lambda_init, lam=lw["lam"], gain=lw["subln_g"])
        mla_o = _attention(mlaq, mlak, mlav, geo, heads=MLA_HEADS, n_maps=1, dv=MLA_V, need_ctx=need_ctx)
        hy_o = _hy_conv(hy_p, lw["hy"], _hy_filter(lw["hy"], consts_lat), consts_lat, geo, ctx=False)
        if need_ctx:
            hy_c = _hy_conv(hy_p, lw["hy"], _hy_filter(lw["hy"], consts_ctx), consts_ctx, geo, ctx=True)
            hy_o = jnp.concatenate([hy_o, hy_c], axis=0)

        idx = l // 2
        if l % 2 == 0:
            x1, h2 = _out_proj(da_o, hy_o, mla_o, xs, mod, lw, geo, t_rows=t_out, alpha=alpha)
            xs = _ffn_dense(h2, ffn_w_gate[idx:idx + 1].astype(BF16), ffn_w_up[idx:idx + 1].astype(BF16),
                            ffn_w_down[idx:idx + 1].astype(BF16), x1, mod, lw, geo, alpha=alpha)
        else:
            router = jnp.pad(moe_router[idx], ((0, 0), (0, LANES - N_EXPERTS))).astype(BF16)
            x1, h2, route = _out_proj(da_o, hy_o, mla_o, xs, mod, lw, geo, t_rows=t_out, alpha=alpha,
                                      router=router)
            lw["moe_wg"] = moe_w_gate[idx].astype(BF16)
            lw["moe_wu"] = moe_w_up[idx].astype(BF16)
            lw["moe_wd"] = moe_w_down[idx].astype(BF16)
            xs = _moe_layer(x1, h2, route, mod, lw, geo, alpha=alpha)
    return xs[:t_lat].reshape(b, nl, d)
```

```python
import functools
import math

import jax
import jax.numpy as jnp
from jax import lax
from jax.experimental import pallas as pl
from jax.experimental.pallas import tpu as pltpu

F32 = jnp.float32
BF16 = jnp.bfloat16

GRID_W = 64
DA_HEADS = 4
DA_HEAD_DIM = 48
DA_V_DIM = 2 * DA_HEAD_DIM
HY_CH = 256
HY_ORDER = 2
HY_EMB = 33
HY_HIDDEN = 64
HY_DECAY_TARGET = 1e-2
HY_DECAY_SHORT_PCT = 0.3
HY_DECAY_LONG_PCT = 1.5
HY_DECAY_SHIFT = 0.05
MLA_HEADS = 6
MLA_Q_RANK = 256
MLA_KV_RANK = 128
MLA_NOPE = 64
MLA_ROPE = 32
MLA_V = 64
N_EXPERTS = 8
ROPE_BASE = 10000.0
LN_EPS = 1e-5
RMS_EPS = 1e-6

LANES = 128
VMEM_LIMIT = 56 * 1024 * 1024

DA_Q = DA_HEADS * 2 * DA_HEAD_DIM
DA_SLOTS = DA_HEADS * LANES
MLA_SLOTS = MLA_HEADS * LANES
HY_IN = (HY_ORDER + 1) * HY_CH
MIX_EXT = DA_SLOTS + HY_CH + MLA_SLOTS

_C_Q, _C_QR, _C_K, _C_KR, _C_V = 0, 512, 1024, 1536, 2048
_C_HY = 2560
_C_CQ = _C_HY + HY_IN
_C_CKV = _C_CQ + MLA_Q_RANK
_C_KROPE = _C_CKV + MLA_KV_RANK
IN_EXT = _C_KROPE + LANES


def _cparams(sem):
    return pltpu.CompilerParams(dimension_semantics=sem, vmem_limit_bytes=VMEM_LIMIT)


def _resident(shape):
    nd = len(shape)
    return pl.BlockSpec(shape, lambda *_: (0,) * nd, pipeline_mode=pl.Buffered(1))


def _dot(a, b):
    return jnp.dot(a, b, preferred_element_type=F32)


def _dot_nt(a, b):
    return lax.dot_general(a, b, (((1,), (1,)), ((), ())), preferred_element_type=F32)


def _layer_norm_rows(y, g, b):
    mu = jnp.mean(y, axis=-1, keepdims=True)
    yc = y - mu
    var = jnp.mean(yc * yc, axis=-1, keepdims=True)
    return yc * lax.rsqrt(var + LN_EPS) * g + b


def _sigmoid(x):
    return 1.0 / (1.0 + jnp.exp(-x))


def _ada_kernel(c_ref, w_ref, b_ref, o_ref):
    c = c_ref[...]
    act = (c * _sigmoid(c)).astype(BF16)
    o_ref[0] = _dot(act, w_ref[0].astype(BF16)) + b_ref[0]


def _ada_all(c_all, ada_w, ada_b):
    depth, d, n6 = ada_w.shape
    bp = c_all.shape[0]
    tn = 512
    return pl.pallas_call(
        _ada_kernel,
        grid=(depth, n6 // tn),
        in_specs=[
            pl.BlockSpec((bp, d), lambda l, j: (0, 0)),
            pl.BlockSpec((1, d, tn), lambda l, j: (l, 0, j)),
            pl.BlockSpec((1, 1, tn), lambda l, j: (l, 0, j)),
        ],
        out_specs=pl.BlockSpec((1, bp, tn), lambda l, j: (l, 0, j)),
        out_shape=jax.ShapeDtypeStruct((depth, bp, n6), F32),
        compiler_params=_cparams(("parallel", "parallel")),
        name="ada_mod",
    )(c_all, ada_w, ada_b.reshape(depth, 1, n6))


def _tile_lanes(x, reps):
    return jnp.concatenate([x] * reps, axis=-1)


def _rms_rows(x, g):
    return x * lax.rsqrt(jnp.mean(x * x, axis=-1, keepdims=True) + RMS_EPS) * g


def _in_proj_kernel(x_ref, mod_ref, w_ref, tab_ref, qg_ref, kvg_ref, wuq_ref, wuqr_ref, wukv_ref,
                    e2_ref, vbda_ref, vbmla_ref,
                    daq_ref, dak_ref, dav_ref, hy_ref, mlaq_ref, mlak_ref, mlav_ref,
                    *, da_scale, mla_scale):
    m = mod_ref[0]
    h = (x_ref[...] * (1.0 + m[1:2]) + m[0:1]).astype(BF16)

    def seg(a, b):
        return _dot(h, w_ref[:, a:b])

    cos_da = _tile_lanes(tab_ref[:, 0:128], DA_HEADS)
    sin_da = _tile_lanes(tab_ref[:, 128:256], DA_HEADS)
    daq_ref[...] = ((seg(_C_Q, _C_QR) * cos_da + seg(_C_QR, _C_K) * sin_da) * da_scale).astype(BF16)
    dak_ref[...] = (seg(_C_K, _C_KR) * cos_da + seg(_C_KR, _C_V) * sin_da).astype(BF16)
    dav_ref[...] = (seg(_C_V, _C_HY) + vbda_ref[...]).astype(BF16)
    hy_ref[...] = seg(_C_HY, _C_CQ)

    cqn = _rms_rows(seg(_C_CQ, _C_CKV), qg_ref[...]).astype(BF16)
    cos_mq = _tile_lanes(tab_ref[:, 256:384], MLA_HEADS)
    sin_mq = _tile_lanes(tab_ref[:, 384:512], MLA_HEADS)
    mlaq_ref[...] = ((_dot(cqn, wuq_ref[...]) * cos_mq + _dot(cqn, wuqr_ref[...]) * sin_mq)
                     * mla_scale).astype(BF16)

    ckvn = _rms_rows(seg(_C_CKV, _C_KROPE), kvg_ref[...]).astype(BF16)
    kv = _dot(ckvn, wukv_ref[...])
    krw = seg(_C_KROPE, IN_EXT) * tab_ref[:, 512:640]
    kr_hi = krw.astype(BF16)
    kr_lo = (krw - kr_hi.astype(F32)).astype(BF16)
    placed = _dot(kr_hi, e2_ref[...]) + _dot(kr_lo, e2_ref[...])
    mlak_ref[...] = (kv[:, :MLA_SLOTS] + placed).astype(BF16)
    mlav_ref[...] = (kv[:, MLA_SLOTS:] + vbmla_ref[...]).astype(BF16)


def _in_proj(xs, mod, lw, tab, geo):
    t_rows, d = xs.shape
    tm = geo["tm"]
    row = lambda i: (i, 0)
    outs = [(DA_SLOTS, BF16), (DA_SLOTS, BF16), (DA_SLOTS, BF16), (HY_IN, F32),
            (MLA_SLOTS, BF16), (MLA_SLOTS, BF16), (MLA_SLOTS, BF16)]
    kern = functools.partial(_in_proj_kernel, da_scale=DA_HEAD_DIM ** -0.5,
                             mla_scale=(MLA_NOPE + MLA_ROPE) ** -0.5)
    return pl.pallas_call(
        kern,
        grid=(t_rows // tm,),
        in_specs=[
            pl.BlockSpec((tm, d), row),
            pl.BlockSpec((1, 6, d), geo["mod_map"]),
            _resident((d, IN_EXT)),
            pl.BlockSpec((tm, 5 * LANES), geo["tab_map"]),
            _resident((1, MLA_Q_RANK)),
            _resident((1, MLA_KV_RANK)),
            _resident((MLA_Q_RANK, MLA_SLOTS)),
            _resident((MLA_Q_RANK, MLA_SLOTS)),
            _resident((MLA_KV_RANK, 2 * MLA_SLOTS)),
            _resident((LANES, MLA_SLOTS)),
            _resident((1, DA_SLOTS)),
            _resident((1, MLA_SLOTS)),
        ],
        out_specs=[pl.BlockSpec((tm, w), row) for w, _ in outs],
        out_shape=[jax.ShapeDtypeStruct((t_rows, w), dt) for w, dt in outs],
        compiler_params=_cparams(("parallel",)),
        name="in_proj",
    )(xs, mod, lw["w_in"], tab, lw["qg"], lw["kvg"], lw["wuq"], lw["wuqr"], lw["wukv"],
      lw["e2"], lw["vb_da"], lw["vb_mla"])


def _attn_kernel(*refs, n_maps, dv, post_scale, lat_queries, hps):
    if n_maps == 2:
        lam_ref, g_ref, q_ref, kl_ref, kc_ref, vl_ref, vc_ref, o_ref = refs
    else:
        q_ref, kl_ref, kc_ref, vl_ref, vc_ref, o_ref = refs
    tq = q_ref.shape[0]
    lane = lax.broadcasted_iota(jnp.int32, (tq, LANES), 1)

    def attend(qm, hs, include_lat):
        sc = _dot_nt(qm, kc_ref[:, hs])
        mx = jnp.max(sc, axis=-1, keepdims=True)
        if include_lat:
            sl = _dot_nt(qm, kl_ref[:, hs])
            mx = jnp.maximum(mx, jnp.max(sl, axis=-1, keepdims=True))
            ol = _dot(jnp.exp(sl - mx).astype(BF16), vl_ref[:, hs])
        ol_c = _dot(jnp.exp(sc - mx).astype(BF16), vc_ref[:, hs])
        ol = ol + ol_c if include_lat else ol_c
        den = jnp.sum(jnp.where(lane == dv, ol, 0.0), axis=-1, keepdims=True)
        return ol / den

    def head(h, include_lat):
        hs = slice(h * LANES, (h + 1) * LANES)
        q = q_ref[:, hs]
        if n_maps == 2:
            q1 = jnp.where(lane < DA_HEAD_DIM, q, jnp.zeros_like(q))
            q2 = jnp.where(lane < DA_HEAD_DIM, jnp.zeros_like(q), q)
            lp = lam_ref[...]
            lam = (jnp.exp(jnp.sum(lp[0:1] * lp[1:2], axis=-1, keepdims=True))
                   - jnp.exp(jnp.sum(lp[2:3] * lp[3:4], axis=-1, keepdims=True))
                   + (1.0 - post_scale))
            o = attend(q1, hs, include_lat) - lam * attend(q2, hs, include_lat)
            o = jnp.where(lane < dv, o, 0.0)
            ms = jnp.sum(o * o, axis=-1, keepdims=True) * (1.0 / dv)
            return o * lax.rsqrt(ms + RMS_EPS) * g_ref[...] * post_scale
        return jnp.where(lane < dv, attend(q, hs, include_lat), 0.0)

    o = jnp.concatenate([head(h, lat_queries) for h in range(hps)], axis=-1)
    o_ref[...] = o.astype(o_ref.dtype)


def _attention(q, k, v, geo, *, heads, hps, n_maps, dv, lat_queries, post_scale=1.0, lam=None, gain=None):
    b, nl, nc = geo["b"], geo["nl"], geo["nc"]
    tq = geo["tq"] if lat_queries else nc
    n_q_tiles = (nl if lat_queries else nc) // tq
    q_blk0 = 0 if lat_queries else b * nl // nc
    ctx_kv0 = b * nl // nc
    width = hps * LANES
    kv_lat = pl.BlockSpec((nl if lat_queries else 8, width), lambda bi, hi, j: (bi if lat_queries else 0, hi))
    kv_ctx = pl.BlockSpec((nc, width), lambda bi, hi, j: (ctx_kv0 + bi, hi))
    in_specs = [pl.BlockSpec((tq, width), lambda bi, hi, j: (q_blk0 + bi * n_q_tiles + j, hi)),
                kv_lat, kv_ctx, kv_lat, kv_ctx]
    args = [q, k, k, v, v]
    if n_maps == 2:
        in_specs = [pl.BlockSpec((4, LANES), lambda bi, hi, j: (0, 0)),
                    pl.BlockSpec((1, LANES), lambda bi, hi, j: (0, 0))] + in_specs
        args = [lam, gain] + args
    kern = functools.partial(_attn_kernel, n_maps=n_maps, dv=dv, post_scale=post_scale,
                             lat_queries=lat_queries, hps=hps)
    return pl.pallas_call(
        kern,
        grid=(b, heads // hps, n_q_tiles),
        in_specs=in_specs,
        out_specs=pl.BlockSpec((tq, width), lambda bi, hi, j: (bi * n_q_tiles + j, hi)),
        out_shape=jax.ShapeDtypeStruct((b * n_q_tiles * tq, q.shape[1]), BF16),
        compiler_params=_cparams(("parallel", "parallel", "arbitrary")),
        name=("diff_attn" if n_maps == 2 else "mla_attn") + ("_lat" if lat_queries else "_ctx"),
    )(*args)


def _hy_filter_kernel(z_ref, w1_ref, b1_ref, w2_ref, b2_ref, w3_ref, b3_ref, w4_ref, fr_ref,
                      win_ref, sgn_ref, a_ref, d_ref, hn_ref, *, rows):
    hp = lax.Precision.HIGHEST
    fr = fr_ref[...]

    def lin(x, w_ref, b_ref):
        return jnp.dot(x, w_ref[...], precision=hp, preferred_element_type=F32) + b_ref[...]

    hdn = jnp.sin(fr * lin(z_ref[...], w1_ref, b1_ref))
    hdn = jnp.sin(fr * lin(hdn, w2_ref, b2_ref))
    hdn = jnp.sin(fr * lin(hdn, w3_ref, b3_ref))
    filt = jnp.dot(hdn, w4_ref[...], precision=hp, preferred_element_type=F32)
    win = _tile_lanes(win_ref[...], HY_ORDER)
    half = HY_ORDER * HY_CH
    fwd = filt[:, :half] * win
    bwd = filt[:, half:] * win
    i = pl.program_id(0)
    row = lax.broadcasted_iota(jnp.int32, bwd.shape, 0) + i * rows
    bwd = jnp.where(row == 0, 0.0, bwd)
    a = fwd + bwd
    a_ref[...] = a.astype(BF16)
    d_ref[...] = (bwd - fwd).astype(BF16)

    @pl.when(i == 0)
    def _():
        hn_ref[...] = jnp.zeros_like(hn_ref)

    hn_ref[...] += jnp.sum(a * sgn_ref[...], axis=0, keepdims=True)


def _hy_spectrum_kernel(c_ref, s_ref, a_ref, d_ref, hr_ref, hi_ref):
    hr_ref[...] = _dot(c_ref[...], a_ref[...])
    hi_ref[...] = _dot(s_ref[...], d_ref[...])


def _hy_filter(hw, consts):
    n = consts["c"].shape[0]
    half = HY_ORDER * HY_CH
    tr = _pick_tile(512, n)
    row = lambda i: (i, 0)
    weights = (hw["fw1"], hw["fb1"], hw["fw2"], hw["fb2"], hw["fw3"], hw["fb3"], hw["fw4"], hw["freq"])
    a, d, hn = pl.pallas_call(
        functools.partial(_hy_filter_kernel, rows=tr),
        grid=(n // tr,),
        in_specs=([pl.BlockSpec((tr, HY_HIDDEN), row)] + [_resident(w.shape) for w in weights]
                  + [pl.BlockSpec((tr, HY_CH), row), pl.BlockSpec((tr, 1), row)]),
        out_specs=[pl.BlockSpec((tr, half), row), pl.BlockSpec((tr, half), row),
                   pl.BlockSpec((1, half), lambda i: (0, 0))],
        out_shape=[jax.ShapeDtypeStruct((n, half), BF16), jax.ShapeDtypeStruct((n, half), BF16),
                   jax.ShapeDtypeStruct((1, half), F32)],
        compiler_params=_cparams(("arbitrary",)),
        name="hyena_filter",
    )(consts["z"], *weights, consts["win"], consts["sgn"])
    col = lambda j: (0, j)
    hr, hi = pl.pallas_call(
        _hy_spectrum_kernel,
        grid=(half // HY_CH,),
        in_specs=[_resident((n, n)), _resident((n, n)), pl.BlockSpec((n, HY_CH), col),
                  pl.BlockSpec((n, HY_CH), col)],
        out_specs=[pl.BlockSpec((n, HY_CH), col), pl.BlockSpec((n, HY_CH), col)],
        out_shape=[jax.ShapeDtypeStruct((n, half), F32), jax.ShapeDtypeStruct((n, half), F32)],
        compiler_params=_cparams(("parallel",)),
        name="hyena_spectrum",
    )(consts["c"], consts["s"], a, d)
    return hr, hi, hn


def _hy_conv_kernel(*refs, n, kc, rc):
    (p_ref, cw_ref, cb_ref, c_ref, s_ref, hr_ref, hi_ref, hn_ref, sgn_ref, bias_ref, o_ref,
     z_s, zb_s, y_s) = refs
    ri = lax.broadcasted_iota(jnp.int32, (rc, HY_CH), 0)
    zero_row = jnp.zeros((1, HY_CH), F32)

    def short_conv(g, r0):
        cs = slice(g * HY_CH, (g + 1) * HY_CH)
        p = p_ref[r0:r0 + rc, cs]
        up = p_ref[r0 - 1:r0, cs] if r0 > 0 else zero_row
        dn = p_ref[r0 + rc:r0 + rc + 1, cs] if r0 + rc < n else zero_row
        prev = jnp.where(ri == 0, up, pltpu.roll(p, 1, 0))
        nxt = jnp.where(ri == rc - 1, dn, pltpu.roll(p, rc - 1, 0))
        return prev * cw_ref[0, :, cs] + p * cw_ref[1, :, cs] + nxt * cw_ref[2, :, cs] + cb_ref[:, cs]

    for r0 in range(0, n, rc):
        z_s[r0:r0 + rc, :] = short_conv(HY_ORDER, r0)
    for i in range(HY_ORDER):
        cs = slice(i * HY_CH, (i + 1) * HY_CH)
        zb_s[...] = z_s[...].astype(BF16)
        xn = jnp.sum(z_s[...] * sgn_ref[...], axis=0, keepdims=True)
        y_s[...] = sgn_ref[...] * (xn * hn_ref[:, cs] * (0.5 / n))
        for k0 in range(0, n, kc):
            ks = slice(k0, k0 + kc)
            zb = zb_s[...]
            a = _dot(c_ref[ks, :], zb)
            bm = _dot(s_ref[ks, :], zb)
            hr = hr_ref[ks, cs]
            hi = hi_ref[ks, cs]
            yr = a * hr + bm * hi
            if k0 == 0:
                yr = jnp.where(lax.broadcasted_iota(jnp.int32, yr.shape, 0) == 0, 0.5 * yr, yr)
            yi = a * hi - bm * hr
            y_s[...] += (_dot(c_ref[:, ks], yr.astype(BF16)) - _dot(s_ref[:, ks], yi.astype(BF16))) * (1.0 / n)
        dst = o_ref if i == HY_ORDER - 1 else z_s
        for r0 in range(0, n, rc):
            rs = slice(r0, r0 + rc)
            dst[rs, :] = short_conv(i, r0) * (y_s[rs, :] + z_s[rs, :] * bias_ref[i])


def _hy_conv(hy_p, hw, spec, consts, geo, *, ctx):
    b, nl, nc = geo["b"], geo["nl"], geo["nc"]
    n = nc if ctx else nl
    blk0 = b * nl // nc if ctx else 0
    hr, hi, hn = spec
    args = [hy_p, hw["conv_w"], hw["conv_b"], consts["c"], consts["s"], hr, hi, hn, consts["sgn"],
            hw["bias"]]
    in_specs = ([pl.BlockSpec((n, HY_IN), lambda bi: (blk0 + bi, 0), pipeline_mode=pl.Buffered(1))]
                + [_resident(a.shape) for a in args[1:]])
    kern = functools.partial(_hy_conv_kernel, n=n, kc=_pick_tile(512, n), rc=_pick_tile(256, n))
    return pl.pallas_call(
        kern,
        grid=(b,),
        in_specs=in_specs,
        out_specs=pl.BlockSpec((n, HY_CH), lambda bi: (bi, 0)),
        out_shape=jax.ShapeDtypeStruct((b * n, HY_CH), F32),
        scratch_shapes=[pltpu.VMEM((n, HY_CH), F32), pltpu.VMEM((n, HY_CH), BF16), pltpu.VMEM((n, HY_CH), F32)],
        compiler_params=_cparams(("parallel",)),
        name="hyena_conv_ctx" if ctx else "hyena_conv_lat",
    )(*args)


def _top2_route(logits):
    lane = lax.broadcasted_iota(jnp.int32, logits.shape, 1)
    neg = jnp.float32(-jnp.inf)
    lg = jnp.where(lane < N_EXPERTS, logits, neg)
    m1 = jnp.max(lg, axis=-1, keepdims=True)
    i1 = jnp.min(jnp.where(lg == m1, lane, LANES), axis=-1, keepdims=True)
    lg2 = jnp.where(lane == i1, neg, lg)
    m2 = jnp.max(lg2, axis=-1, keepdims=True)
    i2 = jnp.min(jnp.where(lg2 == m2, lane, LANES), axis=-1, keepdims=True)
    e = jnp.exp(m2 - m1)
    g1 = 1.0 / (1.0 + e)
    g2 = e * g1
    out = jnp.where(lane == 0, i1.astype(F32), 0.0)
    out = jnp.where(lane == 1, i2.astype(F32), out)
    out = jnp.where(lane == 2, g1, out)
    return jnp.where(lane == 3, g2, out)


def _out_proj_kernel(*refs, alpha, moe):
    if moe:
        (da_ref, hy_ref, mla_ref, w_ref, x_ref, mod_ref, g_ref, b_ref, rw_ref,
         x1_ref, h2_ref, route_ref) = refs
    else:
        da_ref, hy_ref, mla_ref, w_ref, x_ref, mod_ref, g_ref, b_ref, x1_ref, h2_ref = refs
    m = mod_ref[0]
    o = (_dot(da_ref[...], w_ref[0:DA_SLOTS])
         + _dot(hy_ref[...].astype(BF16), w_ref[DA_SLOTS:DA_SLOTS + HY_CH])
         + _dot(mla_ref[...], w_ref[DA_SLOTS + HY_CH:MIX_EXT]))
    x1 = _layer_norm_rows(alpha * x_ref[...] + m[2:3] * o, g_ref[...], b_ref[...])
    x1_ref[...] = x1
    h2 = x1 * (1.0 + m[4:5]) + m[3:4]
    h2_ref[...] = h2.astype(h2_ref.dtype)
    if moe:
        route_ref[...] = _top2_route(_dot(h2.astype(BF16), rw_ref[...]))


def _out_proj(da_o, hy_o, mla_o, xs, mod, lw, geo, *, t_rows, alpha, router=None):
    d = xs.shape[1]
    tm = geo["tm"]
    row = lambda i: (i, 0)
    moe = router is not None
    in_specs = [pl.BlockSpec((tm, DA_SLOTS), row), pl.BlockSpec((tm, HY_CH), row),
                pl.BlockSpec((tm, MLA_SLOTS), row), _resident((MIX_EXT, d)), pl.BlockSpec((tm, d), row),
                pl.BlockSpec((1, 6, d), geo["mod_map"]), _resident((1, d)), _resident((1, d))]
    args = [da_o, hy_o, mla_o, lw["w_out"], xs, mod, lw["ln1_g"], lw["ln1_b"]]
    out_specs = [pl.BlockSpec((tm, d), row), pl.BlockSpec((tm, d), row)]
    out_shape = [jax.ShapeDtypeStruct((t_rows, d), F32),
                 jax.ShapeDtypeStruct((t_rows, d), F32 if moe else BF16)]
    if moe:
        in_specs.append(_resident((d, LANES)))
        args.append(router)
        out_specs.append(pl.BlockSpec((tm, LANES), row))
        out_shape.append(jax.ShapeDtypeStruct((t_rows, LANES), F32))
    return pl.pallas_call(
        functools.partial(_out_proj_kernel, alpha=alpha, moe=moe),
        grid=(t_rows // tm,),
        in_specs=in_specs,
        out_specs=out_specs,
        out_shape=out_shape,
        compiler_params=_cparams(("parallel",)),
        name="out_proj_ln1",
    )(*args)


def _swiglu_kernel(*refs, alpha, grouped):
    if grouped:
        be_ref, na_ref, h_ref, wg_ref, wu_ref, wd_ref, o_ref, acc_ref = refs
    else:
        h_ref, wg_ref, wu_ref, wd_ref, x_ref, mod_ref, g_ref, b_ref, o_ref, acc_ref = refs
    f = pl.program_id(1)
    nf = pl.num_programs(1)

    def step():
        @pl.when(f == 0)
        def _():
            acc_ref[...] = jnp.zeros_like(acc_ref)

        h = h_ref[...].astype(BF16)
        g = _dot(h, wg_ref[0])
        u = _dot(h, wu_ref[0])
        act = (g * _sigmoid(g) * u).astype(BF16)
        acc_ref[...] += _dot(act, wd_ref[0])

    if grouped:
        active = pl.program_id(0) < na_ref[0]
        pl.when(active)(step)

        @pl.when(f == nf - 1)
        def _():
            o_ref[...] = jnp.where(active, acc_ref[...], 0.0)
    else:
        step()

        @pl.when(f == nf - 1)
        def _():
            m = mod_ref[0]
            o_ref[...] = _layer_norm_rows(alpha * x_ref[...] + m[5:6] * acc_ref[...],
                                          g_ref[...], b_ref[...])


def _ffn_dense(h2, wg, wu, wd, x1, mod, lw, geo, *, alpha):
    t_rows, d = h2.shape
    f_dim = wg.shape[-1]
    tm, tf = geo["tm_ffn"], geo["tf"]
    row = lambda i, f: (i, 0)
    mod_map = geo["mod_map_ffn"]
    return pl.pallas_call(
        functools.partial(_swiglu_kernel, alpha=alpha, grouped=False),
        grid=(t_rows // tm, f_dim // tf),
        in_specs=[
            pl.BlockSpec((tm, d), row),
            pl.BlockSpec((1, d, tf), lambda i, f: (0, 0, f)),
            pl.BlockSpec((1, d, tf), lambda i, f: (0, 0, f)),
            pl.BlockSpec((1, tf, d), lambda i, f: (0, f, 0)),
            pl.BlockSpec((tm, d), row),
            pl.BlockSpec((1, 6, d), lambda i, f: mod_map(i)),
            pl.BlockSpec((1, d), lambda i, f: (0, 0)),
            pl.BlockSpec((1, d), lambda i, f: (0, 0)),
        ],
        out_specs=pl.BlockSpec((tm, d), row),
        out_shape=jax.ShapeDtypeStruct((t_rows, d), F32),
        scratch_shapes=[pltpu.VMEM((tm, d), F32)],
        compiler_params=_cparams(("parallel", "arbitrary")),
        name="ffn_swiglu_ln2",
    )(h2, wg, wu, wd, x1, mod, lw["ln2_g"], lw["ln2_b"])


def _ffn_grouped(rows, wg, wu, wd, block_expert, n_active, geo):
    r_rows, d = rows.shape
    f_dim = wg.shape[-1]
    mb, tf = geo["mb"], geo["tf"]
    nf = f_dim // tf

    def f_idx(i, f, na):
        return jnp.where(i < na[0], f, nf - 1)

    grid_spec = pltpu.PrefetchScalarGridSpec(
        num_scalar_prefetch=2,
        grid=(r_rows // mb, nf),
        in_specs=[
            pl.BlockSpec((mb, d), lambda i, f, be, na: (jnp.minimum(i, na[0] - 1), 0)),
            pl.BlockSpec((1, d, tf), lambda i, f, be, na: (be[i], 0, f_idx(i, f, na))),
            pl.BlockSpec((1, d, tf), lambda i, f, be, na: (be[i], 0, f_idx(i, f, na))),
            pl.BlockSpec((1, tf, d), lambda i, f, be, na: (be[i], f_idx(i, f, na), 0)),
        ],
        out_specs=pl.BlockSpec((mb, d), lambda i, f, be, na: (i, 0)),
        scratch_shapes=[pltpu.VMEM((mb, d), F32)],
    )
    return pl.pallas_call(
        functools.partial(_swiglu_kernel, alpha=0.0, grouped=True),
        grid_spec=grid_spec,
        out_shape=jax.ShapeDtypeStruct((r_rows, d), F32),
        compiler_params=_cparams(("arbitrary", "arbitrary")),
        name="moe_experts",
    )(block_expert, n_active, rows, wg, wu, wd)


def _gather_kernel(idx_ref, src_ref, o_ref, sem, *, rows):
    def start(r, carry):
        pltpu.make_async_copy(src_ref.at[pl.ds(idx_ref[r], 1)], o_ref.at[pl.ds(r, 1)], sem).start()
        return carry

    lax.fori_loop(0, rows, start, 0, unroll=8)
    pltpu.make_async_copy(o_ref, o_ref, sem).wait()


def _gather_rows(src, idx, rows_per_step):
    m = idx.shape[0]
    d = src.shape[1]
    return pl.pallas_call(
        functools.partial(_gather_kernel, rows=rows_per_step),
        grid=(m // rows_per_step,),
        in_specs=[pl.BlockSpec((rows_per_step,), lambda i: (i,), memory_space=pltpu.SMEM),
                  pl.BlockSpec(memory_space=pl.ANY)],
        out_specs=pl.BlockSpec((rows_per_step, d), lambda i: (i, 0)),
        out_shape=jax.ShapeDtypeStruct((m, d), src.dtype),
        scratch_shapes=[pltpu.SemaphoreType.DMA(())],
        compiler_params=_cparams(("arbitrary",)),
        name="gather_rows",
    )(idx, src)


def _moe_ln2_kernel(x_ref, y1_ref, y2_ref, route_ref, mod_ref, g_ref, b_ref, o_ref, *, alpha):
    m = mod_ref[0]
    rt = route_ref[...]
    lane = lax.broadcasted_iota(jnp.int32, rt.shape, 1)
    g1 = jnp.sum(jnp.where(lane == 2, rt, 0.0), axis=-1, keepdims=True)
    g2 = jnp.sum(jnp.where(lane == 3, rt, 0.0), axis=-1, keepdims=True)
    f = y1_ref[...] * g1 + y2_ref[...] * g2
    o_ref[...] = _layer_norm_rows(alpha * x_ref[...] + m[5:6] * f, g_ref[...], b_ref[...])


def _moe_ln2(x1, pairs, route, mod, lw, geo, *, alpha):
    t_rows, d = x1.shape
    tm = geo["tm"]
    nt = t_rows // tm
    row = lambda i: (i, 0)
    return pl.pallas_call(
        functools.partial(_moe_ln2_kernel, alpha=alpha),
        grid=(nt,),
        in_specs=[pl.BlockSpec((tm, d), row), pl.BlockSpec((tm, d), row),
                  pl.BlockSpec((tm, d), lambda i: (nt + i, 0)), pl.BlockSpec((tm, LANES), row),
                  pl.BlockSpec((1, 6, d), geo["mod_map"]), _resident((1, d)), _resident((1, d))],
        out_specs=pl.BlockSpec((tm, d), row),
        out_shape=jax.ShapeDtypeStruct((t_rows, d), F32),
        compiler_params=_cparams(("parallel",)),
        name="moe_combine_ln2",
    )(x1, pairs, pairs, route, mod, lw["ln2_g"], lw["ln2_b"])


def _moe_layer(x1, h2, route, mod, lw, geo, *, alpha):
    t_rows = x1.shape[0]
    mb = geo["mb"]
    n_assign = 2 * t_rows
    n_blocks = -(-(n_assign + N_EXPERTS * (mb - 1)) // mb)
    expert_of = route[:, 0:2].astype(jnp.int32).reshape(n_assign)
    onehot = (expert_of[:, None] == jnp.arange(N_EXPERTS, dtype=jnp.int32)[None, :]).astype(jnp.int32)
    csum = jnp.cumsum(onehot, axis=0)
    rank = jnp.take_along_axis(csum, expert_of[:, None], axis=1)[:, 0] - 1
    counts = csum[-1]
    padded = (counts + mb - 1) // mb * mb
    padded_end = jnp.cumsum(padded)
    dest = (padded_end - padded)[expert_of] + rank
    n_active = (padded_end[-1] // mb).astype(jnp.int32).reshape(1)
    block_expert = jnp.minimum(
        jnp.searchsorted(padded_end, jnp.arange(n_blocks, dtype=jnp.int32) * mb, side="right"),
        N_EXPERTS - 1).astype(jnp.int32)
    token_of_row = jnp.zeros((n_blocks * mb,), jnp.int32).at[dest].set(
        jnp.arange(n_assign, dtype=jnp.int32) // 2)
    rows = _gather_rows(h2, token_of_row, mb)
    out = _ffn_grouped(rows, lw["moe_wg"], lw["moe_wu"], lw["moe_wd"], block_expert, n_active, geo)
    pair_idx = jnp.concatenate([dest[0::2], dest[1::2]]).astype(jnp.int32)
    pairs = _gather_rows(out, pair_idx, geo["tm"])
    return _moe_ln2(x1, pairs, route, mod, lw, geo, alpha=alpha)


def _rope_tables(n_lat, tm):
    pos = jnp.arange(n_lat, dtype=jnp.int32)
    row = (pos // GRID_W).astype(F32)
    col = (pos % GRID_W).astype(F32)

    def axial(rot_dim):
        axis_dim = rot_dim // 2
        inv = ROPE_BASE ** (-jnp.arange(0, axis_dim, 2, dtype=F32) / axis_dim)
        ang_r = row[:, None] * inv[None, :]
        ang_c = col[:, None] * inv[None, :]
        ang = jnp.concatenate([ang_r, ang_r, ang_c, ang_c], axis=-1)
        return jnp.cos(ang), jnp.sin(ang)

    def pad(x, width, fill=0.0):
        return jnp.pad(x, ((0, 0), (0, width - x.shape[1])), constant_values=fill)

    cd, sd = axial(DA_HEAD_DIM)
    cm, sm = axial(MLA_ROPE)
    ones_nope = jnp.ones((n_lat, MLA_NOPE), F32)
    zeros_nope = jnp.zeros((n_lat, MLA_NOPE), F32)
    lat = jnp.concatenate([
        pad(jnp.concatenate([cd, cd], -1), LANES), pad(jnp.concatenate([sd, sd], -1), LANES),
        pad(jnp.concatenate([ones_nope, cm], -1), LANES), pad(jnp.concatenate([zeros_nope, sm], -1), LANES),
        pad(jnp.concatenate([cm, sm], -1), LANES)], axis=-1)
    ident_row = jnp.concatenate([
        pad(jnp.ones((1, 2 * DA_HEAD_DIM), F32), LANES), jnp.zeros((1, LANES), F32),
        pad(jnp.ones((1, MLA_NOPE + MLA_ROPE), F32), LANES), jnp.zeros((1, LANES), F32),
        pad(jnp.ones((1, MLA_ROPE), F32), LANES)], axis=-1)
    return jnp.concatenate([lat, jnp.broadcast_to(ident_row, (tm, 5 * LANES))], axis=0)


def _rot_cols(w, dim):
    lead = w.shape[:-1]
    g = w.reshape(lead + (-1, 2, dim // 2))
    return jnp.concatenate([-g[..., 1:2, :], g[..., 0:1, :]], axis=-2).reshape(w.shape)


def _to_slots(w, heads, width):
    lead = w.shape[:-1]
    g = w.reshape(lead + (heads, width))
    g = jnp.pad(g, [(0, 0)] * len(lead) + [(0, 0), (0, LANES - width)])
    return g.reshape(lead + (heads * LANES,))


def _layer_weights(l, p):
    d = p["w_in"].shape[1]
    w_in = p["w_in"][l]
    da = w_in[:, :3 * DA_Q]
    wq, wk, wv = da[:, :DA_Q], da[:, DA_Q:2 * DA_Q], da[:, 2 * DA_Q:]
    half = DA_HEAD_DIM // 2
    hy = w_in[:, 3 * DA_Q:3 * DA_Q + HY_IN]
    mla = w_in[:, 3 * DA_Q + HY_IN:]
    w_cq = mla[:, :MLA_Q_RANK]
    w_ckv = mla[:, MLA_Q_RANK:MLA_Q_RANK + MLA_KV_RANK]
    w_kr = mla[:, MLA_Q_RANK + MLA_KV_RANK:]
    kr_seg = jnp.pad(jnp.concatenate([w_kr, _rot_cols(w_kr, MLA_ROPE // 2)], -1),
                     ((0, 0), (0, LANES - 2 * MLA_ROPE)))
    w_in_ext = jnp.concatenate([
        _to_slots(wq, DA_HEADS, DA_V_DIM), _to_slots(_rot_cols(wq, half), DA_HEADS, DA_V_DIM),
        _to_slots(wk, DA_HEADS, DA_V_DIM), _to_slots(_rot_cols(wk, half), DA_HEADS, DA_V_DIM),
        _to_slots(wv, DA_HEADS, DA_V_DIM), hy, w_cq, w_ckv, kr_seg], axis=-1).astype(BF16)

    qh = MLA_NOPE + MLA_ROPE
    wuq = p["mla_w_uq"][l].reshape(MLA_Q_RANK, MLA_HEADS, qh)
    wuq_rot = jnp.concatenate([jnp.zeros_like(wuq[..., :MLA_NOPE]),
                               _rot_cols(wuq[..., MLA_NOPE:], MLA_ROPE // 2)], -1)
    wukv = p["mla_w_ukv"][l].reshape(MLA_KV_RANK, MLA_HEADS, MLA_NOPE + MLA_V)
    wuk = _to_slots(wukv[..., :MLA_NOPE].reshape(MLA_KV_RANK, -1), MLA_HEADS, MLA_NOPE)
    wuv = _to_slots(wukv[..., MLA_NOPE:].reshape(MLA_KV_RANK, -1), MLA_HEADS, MLA_V)
    lane_src = jnp.arange(LANES)[:, None]
    lane_dst = jnp.arange(MLA_SLOTS)[None, :] % LANES
    e2 = ((lane_src < 2 * MLA_ROPE) & (lane_dst == MLA_NOPE + lane_src % MLA_ROPE)).astype(BF16)

    slot_lane = jnp.arange(DA_SLOTS) % LANES
    vb_da = (slot_lane == DA_V_DIM).astype(F32)[None, :]
    vb_mla = ((jnp.arange(MLA_SLOTS) % LANES) == MLA_V).astype(F32)[None, :]

    w_out = p["w_out"][l]
    w_out_ext = jnp.concatenate([
        _to_slots(w_out[:DA_HEADS * DA_V_DIM].T, DA_HEADS, DA_V_DIM).T,
        w_out[DA_HEADS * DA_V_DIM:DA_HEADS * DA_V_DIM + HY_CH],
        _to_slots(w_out[DA_HEADS * DA_V_DIM + HY_CH:].T, MLA_HEADS, MLA_V).T], axis=0).astype(BF16)

    def pad_lanes(v):
        return jnp.pad(v, (0, LANES - v.shape[0]))[None, :]

    lam = jnp.concatenate([pad_lanes(p[k][l]) for k in
                           ("da_lambda_q1", "da_lambda_k1", "da_lambda_q2", "da_lambda_k2")], axis=0)
    hw = {
        "conv_w": p["hy_conv_w"][l][:, None, :], "conv_b": p["hy_conv_b"][l][None, :],
        "fw1": jnp.pad(p["hy_fw1"][l], ((0, HY_HIDDEN - HY_EMB), (0, 0))), "fb1": p["hy_fb1"][l][None, :],
        "fw2": p["hy_fw2"][l], "fb2": p["hy_fb2"][l][None, :],
        "fw3": p["hy_fw3"][l], "fb3": p["hy_fb3"][l][None, :],
        "fw4": p["hy_fw4"][l], "freq": p["hy_freq"][l][None, :], "bias": p["hy_bias"][l][:, None, :],
    }
    return {
        "w_in": w_in_ext, "qg": p["mla_q_norm_g"][l][None, :], "kvg": p["mla_kv_norm_g"][l][None, :],
        "wuq": _to_slots(wuq.reshape(MLA_Q_RANK, -1), MLA_HEADS, qh).astype(BF16),
        "wuqr": _to_slots(wuq_rot.reshape(MLA_Q_RANK, -1), MLA_HEADS, qh).astype(BF16),
        "wukv": jnp.concatenate([wuk, wuv], -1).astype(BF16), "e2": e2,
        "vb_da": vb_da, "vb_mla": vb_mla, "w_out": w_out_ext, "lam": lam,
        "subln_g": pad_lanes(p["da_subln_g"][l]), "hy": hw,
        "ln1_g": p["ln1_g"][l][None, :], "ln1_b": p["ln1_b"][l][None, :],
        "ln2_g": p["ln2_g"][l][None, :], "ln2_b": p["ln2_b"][l][None, :],
    }


def _hyena_consts(n):
    t = jnp.linspace(0.0, 1.0, n, dtype=F32)[:, None]
    bands = (HY_EMB - 1) // 2
    w = 2.0 * math.pi * jnp.arange(n, dtype=F32)[:, None] / n
    f = jnp.linspace(1e-4, bands - 1, bands, dtype=F32)[None, :]
    z = jnp.concatenate([t, jnp.cos(f * w), -jnp.sin(f * w)], axis=-1)
    z = jnp.pad(z, ((0, 0), (0, HY_HIDDEN - HY_EMB)))
    max_decay = math.log(HY_DECAY_TARGET) / HY_DECAY_SHORT_PCT
    min_decay = math.log(HY_DECAY_TARGET) / HY_DECAY_LONG_PCT
    deltas = jnp.linspace(min_decay, max_decay, HY_CH, dtype=F32)
    win = jnp.exp(-t * jnp.abs(deltas)[None, :]) + HY_DECAY_SHIFT
    k = jnp.arange(n, dtype=jnp.int32)
    ang = ((k[:, None] * k[None, :]) % (2 * n)).astype(F32) * (math.pi / n)
    sgn = (1 - 2 * (k % 2)).astype(F32)[:, None]
    return {"z": z, "win": win, "sgn": sgn, "c": jnp.cos(ang).astype(BF16), "s": jnp.sin(ang).astype(BF16)}


def _pick_tile(limit, *sizes):
    t = limit
    while any(s % t for s in sizes):
        t //= 2
    return t


def _make_geo(b, nl, nc):
    t_lat = b * nl
    tm = _pick_tile(512, nl, b * nc)
    tm_ffn = _pick_tile(512, nl, b * nc)
    geo = {"b": b, "nl": nl, "nc": nc, "tm": tm, "tm_ffn": tm_ffn, "tf": 1408,
           "tq": _pick_tile(512, nl), "mb": 512, "hps_da": 4, "hps_mla": 3}
    n_lat_t, per_b = t_lat // tm, nl // tm
    geo["mod_map"] = lambda i: (jnp.where(i < n_lat_t, i // per_b, b), 0, 0)
    geo["tab_map"] = lambda i: (jnp.where(i < n_lat_t, i % per_b, per_b), 0)
    n_lat_f, per_b_f = t_lat // tm_ffn, nl // tm_ffn
    geo["mod_map_ffn"] = lambda i: (jnp.where(i < n_lat_f, i // per_b_f, b), 0, 0)
    return geo


def kernel(x, c, ctx, c_ctx, ada_w, ada_b, w_in, da_lambda_q1, da_lambda_k1, da_lambda_q2, da_lambda_k2, da_subln_g, hy_conv_w, hy_conv_b, hy_fw1, hy_fb1, hy_fw2, hy_fb2, hy_fw3, hy_fb3, hy_fw4, hy_freq, hy_bias, mla_q_norm_g, mla_w_uq, mla_kv_norm_g, mla_w_ukv, w_out, ln1_g, ln1_b, ln2_g, ln2_b, ffn_w_gate, ffn_w_up, ffn_w_down, moe_router, moe_w_gate, moe_w_up, moe_w_down):
    p = dict(w_in=w_in, da_lambda_q1=da_lambda_q1, da_lambda_k1=da_lambda_k1, da_lambda_q2=da_lambda_q2,
             da_lambda_k2=da_lambda_k2, da_subln_g=da_subln_g, hy_conv_w=hy_conv_w, hy_conv_b=hy_conv_b,
             hy_fw1=hy_fw1, hy_fb1=hy_fb1, hy_fw2=hy_fw2, hy_fb2=hy_fb2, hy_fw3=hy_fw3, hy_fb3=hy_fb3,
             hy_fw4=hy_fw4, hy_freq=hy_freq, hy_bias=hy_bias, mla_q_norm_g=mla_q_norm_g, mla_w_uq=mla_w_uq,
             mla_kv_norm_g=mla_kv_norm_g, mla_w_ukv=mla_w_ukv, w_out=w_out, ln1_g=ln1_g, ln1_b=ln1_b,
             ln2_g=ln2_g, ln2_b=ln2_b)
    b, nl, d = x.shape
    nc = ctx.shape[1]
    depth = ada_w.shape[0]
    assert nl % nc == 0 and nl % GRID_W == 0 and nc % 8 == 0
    alpha = (2 * depth) ** 0.25
    t_lat, t_all = b * nl, b * (nl + nc)

    geo = _make_geo(b, nl, nc)
    tm = geo["tm"]

    bp = -(-(b + 1) // 8) * 8
    c_all = jnp.zeros((bp, d), F32).at[:b].set(c).at[b].set(c_ctx)
    mod_all = _ada_all(c_all, ada_w, ada_b).reshape(depth, bp, 6, d)

    tab = _rope_tables(nl, tm)
    consts_lat = _hyena_consts(nl)
    consts_ctx = _hyena_consts(nc)
    xs = jnp.concatenate([x.reshape(t_lat, d), ctx.reshape(b * nc, d)], axis=0)

    for l in range(depth):
        need_ctx = l < depth - 1
        t_out = t_all if need_ctx else t_lat
        lw = _layer_weights(l, p)
        mod = mod_all[l]
        lambda_init = 0.8 - 0.6 * math.exp(-0.3 * l)

        daq, dak, dav, hy_p, mlaq, mlak, mlav = _in_proj(xs, mod, lw, tab, geo)
        da_kw = dict(heads=DA_HEADS, hps=geo["hps_da"], n_maps=2, dv=DA_V_DIM, post_scale=1.0 - lambda_init,
                     lam=lw["lam"], gain=lw["subln_g"])
        mla_kw = dict(heads=MLA_HEADS, hps=geo["hps_mla"], n_maps=1, dv=MLA_V)
        da_o = _attention(daq, dak, dav, geo, lat_queries=True, **da_kw)
        mla_o = _attention(mlaq, mlak, mlav, geo, lat_queries=True, **mla_kw)
        hy_o = _hy_conv(hy_p, lw["hy"], _hy_filter(lw["hy"], consts_lat), consts_lat, geo, ctx=False)
        if need_ctx:
            hy_c = _hy_conv(hy_p, lw["hy"], _hy_filter(lw["hy"], consts_ctx), consts_ctx, geo, ctx=True)
            hy_o = jnp.concatenate([hy_o, hy_c], axis=0)
            da_o = jnp.concatenate([da_o, _attention(daq, dak, dav, geo, lat_queries=False, **da_kw)], axis=0)
            mla_o = jnp.concatenate([mla_o, _attention(mlaq, mlak, mlav, geo, lat_queries=False, **mla_kw)], axis=0)

        idx = l // 2
        if l % 2 == 0:
            x1, h2 = _out_proj(da_o, hy_o, mla_o, xs, mod, lw, geo, t_rows=t_out, alpha=alpha)
            xs = _ffn_dense(h2, ffn_w_gate[idx:idx + 1].astype(BF16), ffn_w_up[idx:idx + 1].astype(BF16),
                            ffn_w_down[idx:idx + 1].astype(BF16), x1, mod, lw, geo, alpha=alpha)
        else:
            router = jnp.pad(moe_router[idx], ((0, 0), (0, LANES - N_EXPERTS))).astype(BF16)
            x1, h2, route = _out_proj(da_o, hy_o, mla_o, xs, mod, lw, geo, t_rows=t_out, alpha=alpha,
                                      router=router)
            lw["moe_wg"] = moe_w_gate[idx].astype(BF16)
            lw["moe_wu"] = moe_w_up[idx].astype(BF16)
            lw["moe_wd"] = moe_w_down[idx].astype(BF16)
            xs = _moe_layer(x1, h2, route, mod, lw, geo, alpha=alpha)
    return xs[:t_lat].reshape(b, nl, d)
```

```python
import functools
import math

import jax
import jax.numpy as jnp
from jax import lax
from jax.experimental import pallas as pl
from jax.experimental.pallas import tpu as pltpu

F32 = jnp.float32
BF16 = jnp.bfloat16

GRID_W = 64
DA_HEADS = 4
DA_HEAD_DIM = 48
DA_V_DIM = 2 * DA_HEAD_DIM
HY_CH = 256
HY_ORDER = 2
HY_EMB = 33
HY_HIDDEN = 64
HY_DECAY_TARGET = 1e-2
HY_DECAY_SHORT_PCT = 0.3
HY_DECAY_LONG_PCT = 1.5
HY_DECAY_SHIFT = 0.05
MLA_HEADS = 6
MLA_Q_RANK = 256
MLA_KV_RANK = 128
MLA_NOPE = 64
MLA_ROPE = 32
MLA_V = 64
N_EXPERTS = 8
ROPE_BASE = 10000.0
LN_EPS = 1e-5
RMS_EPS = 1e-6

LANES = 128
TOKEN_TILE = 8
VMEM_LIMIT = 56 * 1024 * 1024

DA_Q = DA_HEADS * 2 * DA_HEAD_DIM
DA_SLOTS = DA_HEADS * LANES
MLA_SLOTS = MLA_HEADS * LANES
HY_IN = (HY_ORDER + 1) * HY_CH
MIX_EXT = DA_SLOTS + HY_CH + MLA_SLOTS

_C_Q, _C_QR, _C_K, _C_KR, _C_V = 0, 512, 1024, 1536, 2048
_C_HY = 2560
_C_CQ = _C_HY + HY_IN
_C_CKV = _C_CQ + MLA_Q_RANK
_C_KROPE = _C_CKV + MLA_KV_RANK
IN_EXT = _C_KROPE + LANES


def _cparams(sem):
    return pltpu.CompilerParams(dimension_semantics=sem, vmem_limit_bytes=VMEM_LIMIT)


def _resident(shape):
    nd = len(shape)
    return pl.BlockSpec(shape, lambda *_: (0,) * nd, pipeline_mode=pl.Buffered(1))


def _dot(a, b):
    return jnp.dot(a, b, preferred_element_type=F32)


def _dot_nt(a, b):
    return lax.dot_general(a, b, (((1,), (1,)), ((), ())), preferred_element_type=F32)


def _layer_norm_rows(y, g, b):
    mu = jnp.mean(y, axis=-1, keepdims=True)
    yc = y - mu
    var = jnp.mean(yc * yc, axis=-1, keepdims=True)
    return yc * lax.rsqrt(var + LN_EPS) * g + b


def _sigmoid(x):
    return 1.0 / (1.0 + jnp.exp(-x))


def _ada_kernel(c_ref, w_ref, b_ref, o_ref):
    c = c_ref[...]
    act = (c * _sigmoid(c)).astype(BF16)
    o_ref[0] = _dot(act, w_ref[0].astype(BF16)) + b_ref[0]


def _ada_all(c_all, ada_w, ada_b):
    depth, d, n6 = ada_w.shape
    bp = c_all.shape[0]
    tn = 512
    return pl.pallas_call(
        _ada_kernel,
        grid=(depth, n6 // tn),
        in_specs=[
            pl.BlockSpec((bp, d), lambda l, j: (0, 0)),
            pl.BlockSpec((1, d, tn), lambda l, j: (l, 0, j)),
            pl.BlockSpec((1, 1, tn), lambda l, j: (l, 0, j)),
        ],
        out_specs=pl.BlockSpec((1, bp, tn), lambda l, j: (l, 0, j)),
        out_shape=jax.ShapeDtypeStruct((depth, bp, n6), F32),
        compiler_params=_cparams(("parallel", "parallel")),
        name="ada_mod",
    )(c_all, ada_w, ada_b.reshape(depth, 1, n6))


def _tile_lanes(x, reps):
    return jnp.concatenate([x] * reps, axis=-1)


def _rms_rows(x, g):
    return x * lax.rsqrt(jnp.mean(x * x, axis=-1, keepdims=True) + RMS_EPS) * g


def _in_proj_kernel(x_ref, mod_ref, w_ref, tab_ref, qg_ref, kvg_ref, wuq_ref, wuqr_ref, wukv_ref,
                    e2_ref, vbda_ref, vbmla_ref,
                    daq_ref, dak_ref, dav_ref, hy_ref, mlaq_ref, mlak_ref, mlav_ref,
                    *, da_scale, mla_scale):
    m = mod_ref[0]
    h = (x_ref[...] * (1.0 + m[1:2]) + m[0:1]).astype(BF16)

    def seg(a, b):
        return _dot(h, w_ref[:, a:b])

    cos_da = _tile_lanes(tab_ref[:, 0:128], DA_HEADS)
    sin_da = _tile_lanes(tab_ref[:, 128:256], DA_HEADS)
    daq_ref[...] = ((seg(_C_Q, _C_QR) * cos_da + seg(_C_QR, _C_K) * sin_da) * da_scale).astype(BF16)
    dak_ref[...] = (seg(_C_K, _C_KR) * cos_da + seg(_C_KR, _C_V) * sin_da).astype(BF16)
    dav_ref[...] = (seg(_C_V, _C_HY) + vbda_ref[...]).astype(BF16)
    hy_ref[...] = seg(_C_HY, _C_CQ)

    cqn = _rms_rows(seg(_C_CQ, _C_CKV), qg_ref[...]).astype(BF16)
    cos_mq = _tile_lanes(tab_ref[:, 256:384], MLA_HEADS)
    sin_mq = _tile_lanes(tab_ref[:, 384:512], MLA_HEADS)
    mlaq_ref[...] = ((_dot(cqn, wuq_ref[...]) * cos_mq + _dot(cqn, wuqr_ref[...]) * sin_mq)
                     * mla_scale).astype(BF16)

    ckvn = _rms_rows(seg(_C_CKV, _C_KROPE), kvg_ref[...]).astype(BF16)
    kv = _dot(ckvn, wukv_ref[...])
    krw = seg(_C_KROPE, IN_EXT) * tab_ref[:, 512:640]
    kr_hi = krw.astype(BF16)
    kr_lo = (krw - kr_hi.astype(F32)).astype(BF16)
    placed = _dot(kr_hi, e2_ref[...]) + _dot(kr_lo, e2_ref[...])
    mlak_ref[...] = (kv[:, :MLA_SLOTS] + placed).astype(BF16)
    mlav_ref[...] = (kv[:, MLA_SLOTS:] + vbmla_ref[...]).astype(BF16)


def _in_proj(xs, mod, lw, tab, geo):
    t_rows, d = xs.shape
    tm = geo["tm"]
    row = lambda i: (i, 0)
    outs = [(DA_SLOTS, BF16), (DA_SLOTS, BF16), (DA_SLOTS, BF16), (HY_IN, F32),
            (MLA_SLOTS, BF16), (MLA_SLOTS, BF16), (MLA_SLOTS, BF16)]
    kern = functools.partial(_in_proj_kernel, da_scale=DA_HEAD_DIM ** -0.5,
                             mla_scale=(MLA_NOPE + MLA_ROPE) ** -0.5)
    return pl.pallas_call(
        kern,
        grid=(t_rows // tm,),
        in_specs=[
            pl.BlockSpec((tm, d), row),
            pl.BlockSpec((1, 6, d), geo["mod_map"]),
            _resident((d, IN_EXT)),
            pl.BlockSpec((tm, 5 * LANES), geo["tab_map"]),
            _resident((1, MLA_Q_RANK)),
            _resident((1, MLA_KV_RANK)),
            _resident((MLA_Q_RANK, MLA_SLOTS)),
            _resident((MLA_Q_RANK, MLA_SLOTS)),
            _resident((MLA_KV_RANK, 2 * MLA_SLOTS)),
            _resident((LANES, MLA_SLOTS)),
            _resident((1, DA_SLOTS)),
            _resident((1, MLA_SLOTS)),
        ],
        out_specs=[pl.BlockSpec((tm, w), row) for w, _ in outs],
        out_shape=[jax.ShapeDtypeStruct((t_rows, w), dt) for w, dt in outs],
        compiler_params=_cparams(("parallel",)),
        name="in_proj",
    )(xs, mod, lw["w_in"], tab, lw["qg"], lw["kvg"], lw["wuq"], lw["wuqr"], lw["wukv"],
      lw["e2"], lw["vb_da"], lw["vb_mla"])


def _attn_kernel(*refs, n_maps, dv, post_scale, lat_queries, hps):
    if n_maps == 2:
        lam_ref, g_ref, q_ref, kl_ref, kc_ref, vl_ref, vc_ref, o_ref = refs
    else:
        q_ref, kl_ref, kc_ref, vl_ref, vc_ref, o_ref = refs
    tq = q_ref.shape[0]
    lane = lax.broadcasted_iota(jnp.int32, (tq, LANES), 1)

    def attend(qm, hs, include_lat):
        sc = _dot_nt(qm, kc_ref[:, hs])
        mx = jnp.max(sc, axis=-1, keepdims=True)
        if include_lat:
            sl = _dot_nt(qm, kl_ref[:, hs])
            mx = jnp.maximum(mx, jnp.max(sl, axis=-1, keepdims=True))
            ol = _dot(jnp.exp(sl - mx).astype(BF16), vl_ref[:, hs])
        ol_c = _dot(jnp.exp(sc - mx).astype(BF16), vc_ref[:, hs])
        ol = ol + ol_c if include_lat else ol_c
        den = jnp.sum(jnp.where(lane == dv, ol, 0.0), axis=-1, keepdims=True)
        return ol / den

    def head(h, include_lat):
        hs = slice(h * LANES, (h + 1) * LANES)
        q = q_ref[:, hs]
        if n_maps == 2:
            q1 = jnp.where(lane < DA_HEAD_DIM, q, jnp.zeros_like(q))
            q2 = jnp.where(lane < DA_HEAD_DIM, jnp.zeros_like(q), q)
            lp = lam_ref[...]
            lam = (jnp.exp(jnp.sum(lp[0:1] * lp[1:2], axis=-1, keepdims=True))
                   - jnp.exp(jnp.sum(lp[2:3] * lp[3:4], axis=-1, keepdims=True))
                   + (1.0 - post_scale))
            o = attend(q1, hs, include_lat) - lam * attend(q2, hs, include_lat)
            o = jnp.where(lane < dv, o, 0.0)
            ms = jnp.sum(o * o, axis=-1, keepdims=True) * (1.0 / dv)
            return o * lax.rsqrt(ms + RMS_EPS) * g_ref[...] * post_scale
        return jnp.where(lane < dv, attend(q, hs, include_lat), 0.0)

    o = jnp.concatenate([head(h, lat_queries) for h in range(hps)], axis=-1)
    o_ref[...] = o.astype(o_ref.dtype)


def _attention(q, k, v, geo, *, heads, hps, n_maps, dv, lat_queries, post_scale=1.0, lam=None, gain=None):
    b, nl, nc = geo["b"], geo["nl"], geo["nc"]
    tq = geo["tq"] if lat_queries else nc
    n_q_tiles = (nl if lat_queries else nc) // tq
    q_blk0 = 0 if lat_queries else b * nl // nc
    ctx_kv0 = b * nl // nc
    width = hps * LANES
    kv_lat = pl.BlockSpec((nl if lat_queries else 8, width), lambda bi, hi, j: (bi if lat_queries else 0, hi))
    kv_ctx = pl.BlockSpec((nc, width), lambda bi, hi, j: (ctx_kv0 + bi, hi))
    in_specs = [pl.BlockSpec((tq, width), lambda bi, hi, j: (q_blk0 + bi * n_q_tiles + j, hi)),
                kv_lat, kv_ctx, kv_lat, kv_ctx]
    args = [q, k, k, v, v]
    if n_maps == 2:
        in_specs = [pl.BlockSpec((4, LANES), lambda bi, hi, j: (0, 0)),
                    pl.BlockSpec((1, LANES), lambda bi, hi, j: (0, 0))] + in_specs
        args = [lam, gain] + args
    kern = functools.partial(_attn_kernel, n_maps=n_maps, dv=dv, post_scale=post_scale,
                             lat_queries=lat_queries, hps=hps)
    return pl.pallas_call(
        kern,
        grid=(b, heads // hps, n_q_tiles),
        in_specs=in_specs,
        out_specs=pl.BlockSpec((tq, width), lambda bi, hi, j: (bi * n_q_tiles + j, hi)),
        out_shape=jax.ShapeDtypeStruct((b * n_q_tiles * tq, q.shape[1]), BF16),
        compiler_params=_cparams(("parallel", "parallel", "arbitrary")),
        name=("diff_attn" if n_maps == 2 else "mla_attn") + ("_lat" if lat_queries else "_ctx"),
    )(*args)


def _hy_filter_kernel(z_ref, w1_ref, b1_ref, w2_ref, b2_ref, w3_ref, b3_ref, w4_ref, fr_ref,
                      win_ref, sgn_ref, a_ref, d_ref, hn_ref, *, rows):
    hp = lax.Precision.HIGHEST
    fr = fr_ref[...]

    def lin(x, w_ref, b_ref):
        return jnp.dot(x, w_ref[...], precision=hp, preferred_element_type=F32) + b_ref[...]

    hdn = jnp.sin(fr * lin(z_ref[...], w1_ref, b1_ref))
    hdn = jnp.sin(fr * lin(hdn, w2_ref, b2_ref))
    hdn = jnp.sin(fr * lin(hdn, w3_ref, b3_ref))
    filt = jnp.dot(hdn, w4_ref[...], precision=hp, preferred_element_type=F32)
    win = _tile_lanes(win_ref[...], HY_ORDER)
    half = HY_ORDER * HY_CH
    fwd = filt[:, :half] * win
    bwd = filt[:, half:] * win
    i = pl.program_id(0)
    row = lax.broadcasted_iota(jnp.int32, bwd.shape, 0) + i * rows
    bwd = jnp.where(row == 0, 0.0, bwd)
    a = fwd + bwd
    a_ref[...] = a.astype(BF16)
    d_ref[...] = (bwd - fwd).astype(BF16)

    @pl.when(i == 0)
    def _():
        hn_ref[...] = jnp.zeros_like(hn_ref)

    hn_ref[...] += jnp.sum(a * sgn_ref[...], axis=0, keepdims=True)


def _hy_spectrum_kernel(c_ref, s_ref, a_ref, d_ref, hr_ref, hi_ref):
    hr_ref[...] = _dot(c_ref[...], a_ref[...])
    hi_ref[...] = _dot(s_ref[...], d_ref[...])


def _hy_filter(hw, consts):
    n = consts["c"].shape[0]
    half = HY_ORDER * HY_CH
    tr = _pick_tile(512, n)
    row = lambda i: (i, 0)
    weights = (hw["fw1"], hw["fb1"], hw["fw2"], hw["fb2"], hw["fw3"], hw["fb3"], hw["fw4"], hw["freq"])
    a, d, hn = pl.pallas_call(
        functools.partial(_hy_filter_kernel, rows=tr),
        grid=(n // tr,),
        in_specs=([pl.BlockSpec((tr, HY_HIDDEN), row)] + [_resident(w.shape) for w in weights]
                  + [pl.BlockSpec((tr, HY_CH), row), pl.BlockSpec((tr, 1), row)]),
        out_specs=[pl.BlockSpec((tr, half), row), pl.BlockSpec((tr, half), row),
                   pl.BlockSpec((1, half), lambda i: (0, 0))],
        out_shape=[jax.ShapeDtypeStruct((n, half), BF16), jax.ShapeDtypeStruct((n, half), BF16),
                   jax.ShapeDtypeStruct((1, half), F32)],
        compiler_params=_cparams(("arbitrary",)),
        name="hyena_filter",
    )(consts["z"], *weights, consts["win"], consts["sgn"])
    col = lambda j: (0, j)
    hr, hi = pl.pallas_call(
        _hy_spectrum_kernel,
        grid=(half // HY_CH,),
        in_specs=[_resident((n, n)), _resident((n, n)), pl.BlockSpec((n, HY_CH), col),
                  pl.BlockSpec((n, HY_CH), col)],
        out_specs=[pl.BlockSpec((n, HY_CH), col), pl.BlockSpec((n, HY_CH), col)],
        out_shape=[jax.ShapeDtypeStruct((n, half), F32), jax.ShapeDtypeStruct((n, half), F32)],
        compiler_params=_cparams(("parallel",)),
        name="hyena_spectrum",
    )(consts["c"], consts["s"], a, d)
    return hr, hi, hn


def _hy_conv_kernel(*refs, n, kc, rc):
    (p_ref, cw_ref, cb_ref, c_ref, s_ref, hr_ref, hi_ref, hn_ref, sgn_ref, bias_ref, o_ref,
     z_s, zb_s, y_s) = refs
    ri = lax.broadcasted_iota(jnp.int32, (rc, HY_CH), 0)
    zero_row = jnp.zeros((1, HY_CH), F32)

    def short_conv(g, r0):
        cs = slice(g * HY_CH, (g + 1) * HY_CH)
        p = p_ref[r0:r0 + rc, cs]
        up = p_ref[r0 - 1:r0, cs] if r0 > 0 else zero_row
        dn = p_ref[r0 + rc:r0 + rc + 1, cs] if r0 + rc < n else zero_row
        prev = jnp.where(ri == 0, up, pltpu.roll(p, 1, 0))
        nxt = jnp.where(ri == rc - 1, dn, pltpu.roll(p, rc - 1, 0))
        return prev * cw_ref[0, :, cs] + p * cw_ref[1, :, cs] + nxt * cw_ref[2, :, cs] + cb_ref[:, cs]

    for r0 in range(0, n, rc):
        z_s[r0:r0 + rc, :] = short_conv(HY_ORDER, r0)
    for i in range(HY_ORDER):
        cs = slice(i * HY_CH, (i + 1) * HY_CH)
        zb_s[...] = z_s[...].astype(BF16)
        xn = jnp.sum(z_s[...] * sgn_ref[...], axis=0, keepdims=True)
        y_s[...] = sgn_ref[...] * (xn * hn_ref[:, cs] * (0.5 / n))
        for k0 in range(0, n, kc):
            ks = slice(k0, k0 + kc)
            zb = zb_s[...]
            a = _dot(c_ref[ks, :], zb)
            bm = _dot(s_ref[ks, :], zb)
            hr = hr_ref[ks, cs]
            hi = hi_ref[ks, cs]
            yr = a * hr + bm * hi
            if k0 == 0:
                yr = jnp.where(lax.broadcasted_iota(jnp.int32, yr.shape, 0) == 0, 0.5 * yr, yr)
            yi = a * hi - bm * hr
            y_s[...] += (_dot(c_ref[:, ks], yr.astype(BF16)) - _dot(s_ref[:, ks], yi.astype(BF16))) * (1.0 / n)
        dst = o_ref if i == HY_ORDER - 1 else z_s
        for r0 in range(0, n, rc):
            rs = slice(r0, r0 + rc)
            dst[rs, :] = short_conv(i, r0) * (y_s[rs, :] + z_s[rs, :] * bias_ref[i])


def _hy_conv(hy_p, hw, spec, consts, geo, *, ctx):
    b, nl, nc = geo["b"], geo["nl"], geo["nc"]
    n = nc if ctx else nl
    blk0 = b * nl // nc if ctx else 0
    hr, hi, hn = spec
    args = [hy_p, hw["conv_w"], hw["conv_b"], consts["c"], consts["s"], hr, hi, hn, consts["sgn"],
            hw["bias"]]
    in_specs = ([pl.BlockSpec((n, HY_IN), lambda bi: (blk0 + bi, 0), pipeline_mode=pl.Buffered(1))]
                + [_resident(a.shape) for a in args[1:]])
    kern = functools.partial(_hy_conv_kernel, n=n, kc=_pick_tile(512, n), rc=_pick_tile(256, n))
    return pl.pallas_call(
        kern,
        grid=(b,),
        in_specs=in_specs,
        out_specs=pl.BlockSpec((n, HY_CH), lambda bi: (bi, 0)),
        out_shape=jax.ShapeDtypeStruct((b * n, HY_CH), F32),
        scratch_shapes=[pltpu.VMEM((n, HY_CH), F32), pltpu.VMEM((n, HY_CH), BF16), pltpu.VMEM((n, HY_CH), F32)],
        compiler_params=_cparams(("parallel",)),
        name="hyena_conv_ctx" if ctx else "hyena_conv_lat",
    )(*args)


def _top2_route(logits):
    lane = lax.broadcasted_iota(jnp.int32, logits.shape, 1)
    neg = jnp.float32(-jnp.inf)
    lg = jnp.where(lane < N_EXPERTS, logits, neg)
    m1 = jnp.max(lg, axis=-1, keepdims=True)
    i1 = jnp.min(jnp.where(lg == m1, lane, LANES), axis=-1, keepdims=True)
    lg2 = jnp.where(lane == i1, neg, lg)
    m2 = jnp.max(lg2, axis=-1, keepdims=True)
    i2 = jnp.min(jnp.where(lg2 == m2, lane, LANES), axis=-1, keepdims=True)
    e = jnp.exp(m2 - m1)
    g1 = 1.0 / (1.0 + e)
    g2 = e * g1
    out = jnp.where(lane == 0, i1.astype(F32), 0.0)
    out = jnp.where(lane == 1, i2.astype(F32), out)
    out = jnp.where(lane == 2, g1, out)
    return jnp.where(lane == 3, g2, out)


def _out_proj_kernel(*refs, alpha, moe, n_lat_tiles, has_ctx):
    n_mix = 6 if has_ctx else 3
    mix_refs, refs = refs[:n_mix], refs[n_mix:]
    if moe:
        w_ref, x_ref, mod_ref, g_ref, b_ref, rw_ref, x1_ref, h2_ref, route_ref = refs[:9]
    else:
        w_ref, x_ref, mod_ref, g_ref, b_ref, x1_ref, h2_ref = refs[:7]

    def mix(da_ref, hy_ref, mla_ref):
        return (_dot(da_ref[...], w_ref[0:DA_SLOTS])
                + _dot(hy_ref[...].astype(BF16), w_ref[DA_SLOTS:DA_SLOTS + HY_CH])
                + _dot(mla_ref[...], w_ref[DA_SLOTS + HY_CH:MIX_EXT]))

    if has_ctx:
        o_ref = refs[-1]
        is_lat = pl.program_id(0) < n_lat_tiles

        @pl.when(is_lat)
        def _():
            o_ref[...] = mix(*mix_refs[:3])

        @pl.when(jnp.logical_not(is_lat))
        def _():
            o_ref[...] = mix(*mix_refs[3:])

        o = o_ref[...]
    else:
        o = mix(*mix_refs)
    m = mod_ref[0]
    x1 = _layer_norm_rows(alpha * x_ref[...] + m[2:3] * o, g_ref[...], b_ref[...])
    x1_ref[...] = x1
    h2 = x1 * (1.0 + m[4:5]) + m[3:4]
    if moe:
        _to_token_tiles(h2_ref, h2)
        route_ref[...] = _top2_route(_dot(h2.astype(BF16), rw_ref[...]))
    else:
        h2_ref[...] = h2.astype(BF16)


def _out_proj(mix_lat, mix_ctx, xs, mod, lw, geo, *, alpha, router=None):
    d = xs.shape[1]
    tm = geo["tm"]
    row = lambda i: (i, 0)
    moe = router is not None
    has_ctx = mix_ctx is not None
    n_lat_tiles = mix_lat[0].shape[0] // tm
    t_rows = mix_lat[0].shape[0] + (mix_ctx[0].shape[0] if has_ctx else 0)
    widths = (DA_SLOTS, HY_CH, MLA_SLOTS)
    lat_row = lambda i: (jnp.minimum(i, n_lat_tiles - 1), 0)
    ctx_row = lambda i: (jnp.maximum(i - n_lat_tiles, 0), 0)
    in_specs = [pl.BlockSpec((tm, w), lat_row) for w in widths]
    args = list(mix_lat)
    if has_ctx:
        in_specs += [pl.BlockSpec((tm, w), ctx_row) for w in widths]
        args += list(mix_ctx)
    in_specs += [_resident((MIX_EXT, d)), pl.BlockSpec((tm, d), row),
                 pl.BlockSpec((1, 6, d), geo["mod_map"]), _resident((1, d)), _resident((1, d))]
    args += [lw["w_out"], xs, mod, lw["ln1_g"], lw["ln1_b"]]
    if moe:
        h2_spec = pl.BlockSpec((tm * TOKEN_TILE, LANES), row)
        h2_shape = jax.ShapeDtypeStruct((t_rows * TOKEN_TILE, LANES), F32)
    else:
        h2_spec, h2_shape = pl.BlockSpec((tm, d), row), jax.ShapeDtypeStruct((t_rows, d), BF16)
    out_specs = [pl.BlockSpec((tm, d), row), h2_spec]
    out_shape = [jax.ShapeDtypeStruct((t_rows, d), F32), h2_shape]
    if moe:
        in_specs.append(_resident((d, LANES)))
        args.append(router)
        out_specs.append(pl.BlockSpec((tm, LANES), row))
        out_shape.append(jax.ShapeDtypeStruct((t_rows, LANES), F32))
    return pl.pallas_call(
        functools.partial(_out_proj_kernel, alpha=alpha, moe=moe, n_lat_tiles=n_lat_tiles, has_ctx=has_ctx),
        grid=(t_rows // tm,),
        in_specs=in_specs,
        out_specs=out_specs,
        out_shape=out_shape,
        scratch_shapes=[pltpu.VMEM((tm, d), F32)] if has_ctx else [],
        compiler_params=_cparams(("parallel",)),
        name="out_proj_ln1",
    )(*args)


def _from_token_tiles(ref, rows):
    return jnp.concatenate([ref[pl.ds(c, rows, stride=TOKEN_TILE), :] for c in range(TOKEN_TILE)], axis=-1)


def _to_token_tiles(ref, val):
    rows = val.shape[0]
    for c in range(TOKEN_TILE):
        ref[pl.ds(c, rows, stride=TOKEN_TILE), :] = val[:, c * LANES:(c + 1) * LANES]


def _swiglu_kernel(*refs, alpha, grouped, fsub):
    if grouped:
        be_ref, na_ref, h_ref, wg_ref, wu_ref, wd_ref, o_ref, act_ref = refs
    else:
        h_ref, wg_ref, wu_ref, wd_ref, x_ref, mod_ref, g_ref, b_ref, o_ref, act_ref = refs

    def body():
        if grouped:
            h = _from_token_tiles(h_ref, act_ref.shape[0]).astype(BF16)
        else:
            h = h_ref[...]
        for c0 in range(0, wg_ref.shape[2], fsub):
            cs = slice(c0, c0 + fsub)
            g = _dot(h, wg_ref[0, :, cs])
            u = _dot(h, wu_ref[0, :, cs])
            act_ref[:, cs] = (g * _sigmoid(g) * u).astype(BF16)
        y = _dot(act_ref[...], wd_ref[0])
        if grouped:
            _to_token_tiles(o_ref, y)
        else:
            m = mod_ref[0]
            o_ref[...] = _layer_norm_rows(alpha * x_ref[...] + m[5:6] * y, g_ref[...], b_ref[...])

    if grouped:
        active = pl.program_id(0) < na_ref[0]
        pl.when(active)(body)

        @pl.when(jnp.logical_not(active))
        def _():
            o_ref[...] = jnp.zeros_like(o_ref)
    else:
        body()


def _ffn_dense(h2, wg, wu, wd, x1, mod, lw, geo, *, alpha):
    t_rows, d = h2.shape
    f_dim = wg.shape[-1]
    tm = geo["tm_ffn"]
    row = lambda i: (i, 0)
    return pl.pallas_call(
        functools.partial(_swiglu_kernel, alpha=alpha, grouped=False, fsub=geo["fsub"]),
        grid=(t_rows // tm,),
        in_specs=[pl.BlockSpec((tm, d), row), _resident((1, d, f_dim)), _resident((1, d, f_dim)),
                  _resident((1, f_dim, d)), pl.BlockSpec((tm, d), row),
                  pl.BlockSpec((1, 6, d), geo["mod_map_ffn"]), _resident((1, d)), _resident((1, d))],
        out_specs=pl.BlockSpec((tm, d), row),
        out_shape=jax.ShapeDtypeStruct((t_rows, d), F32),
        scratch_shapes=[pltpu.VMEM((tm, f_dim), BF16)],
        compiler_params=_cparams(("parallel",)),
        name="ffn_swiglu_ln2",
    )(h2, wg, wu, wd, x1, mod, lw["ln2_g"], lw["ln2_b"])


def _ffn_grouped(rows, wg, wu, wd, block_expert, n_active, geo):
    d, f_dim = wg.shape[1], wg.shape[2]
    mb = geo["mb"]
    r_rows = rows.shape[0] // TOKEN_TILE
    blk = lambda i, be, na: (jnp.minimum(i, na[0] - 1), 0)
    wmap = lambda i, be, na: (be[jnp.minimum(i, na[0] - 1)], 0, 0)
    grid_spec = pltpu.PrefetchScalarGridSpec(
        num_scalar_prefetch=2,
        grid=(r_rows // mb,),
        in_specs=[pl.BlockSpec((mb * TOKEN_TILE, LANES), blk),
                  pl.BlockSpec((1, d, f_dim), wmap), pl.BlockSpec((1, d, f_dim), wmap),
                  pl.BlockSpec((1, f_dim, d), wmap)],
        out_specs=pl.BlockSpec((mb * TOKEN_TILE, LANES), lambda i, be, na: (i, 0)),
        scratch_shapes=[pltpu.VMEM((mb, f_dim), BF16)],
    )
    return pl.pallas_call(
        functools.partial(_swiglu_kernel, alpha=0.0, grouped=True, fsub=geo["fsub"]),
        grid_spec=grid_spec,
        out_shape=jax.ShapeDtypeStruct(rows.shape, F32),
        compiler_params=_cparams(("arbitrary",)),
        name="moe_experts",
    )(block_expert, n_active, rows, wg, wu, wd)


def _gather_kernel(idx_ref, src_ref, o_ref, sem, *, rows):
    def start(r, carry):
        src = pl.multiple_of(idx_ref[r] * TOKEN_TILE, TOKEN_TILE)
        dst = pl.multiple_of(r * TOKEN_TILE, TOKEN_TILE)
        pltpu.make_async_copy(src_ref.at[pl.ds(src, TOKEN_TILE)], o_ref.at[pl.ds(dst, TOKEN_TILE)], sem).start()
        return carry

    lax.fori_loop(0, rows, start, 0, unroll=8)
    pltpu.make_async_copy(o_ref, o_ref, sem).wait()


def _gather_rows(src, idx, rows_per_step):
    m = idx.shape[0]
    return pl.pallas_call(
        functools.partial(_gather_kernel, rows=rows_per_step),
        grid=(m // rows_per_step,),
        in_specs=[pl.BlockSpec((rows_per_step,), lambda i: (i,), memory_space=pltpu.SMEM),
                  pl.BlockSpec(memory_space=pl.ANY)],
        out_specs=pl.BlockSpec((rows_per_step * TOKEN_TILE, LANES), lambda i: (i, 0)),
        out_shape=jax.ShapeDtypeStruct((m * TOKEN_TILE, LANES), src.dtype),
        scratch_shapes=[pltpu.SemaphoreType.DMA(())],
        compiler_params=_cparams(("arbitrary",)),
        name="gather_rows",
    )(idx, src)


def _moe_ln2_kernel(x_ref, y1_ref, y2_ref, route_ref, mod_ref, g_ref, b_ref, o_ref, *, alpha):
    m = mod_ref[0]
    rt = route_ref[...]
    tm = rt.shape[0]
    lane = lax.broadcasted_iota(jnp.int32, rt.shape, 1)
    g1 = jnp.sum(jnp.where(lane == 2, rt, 0.0), axis=-1, keepdims=True)
    g2 = jnp.sum(jnp.where(lane == 3, rt, 0.0), axis=-1, keepdims=True)
    f = _from_token_tiles(y1_ref, tm) * g1 + _from_token_tiles(y2_ref, tm) * g2
    o_ref[...] = _layer_norm_rows(alpha * x_ref[...] + m[5:6] * f, g_ref[...], b_ref[...])


def _moe_ln2(x1, pairs, route, mod, lw, geo, *, alpha):
    t_rows, d = x1.shape
    tm = geo["tm"]
    nt = t_rows // tm
    row = lambda i: (i, 0)
    return pl.pallas_call(
        functools.partial(_moe_ln2_kernel, alpha=alpha),
        grid=(nt,),
        in_specs=[pl.BlockSpec((tm, d), row), pl.BlockSpec((tm * TOKEN_TILE, LANES), row),
                  pl.BlockSpec((tm * TOKEN_TILE, LANES), lambda i: (nt + i, 0)), pl.BlockSpec((tm, LANES), row),
                  pl.BlockSpec((1, 6, d), geo["mod_map"]), _resident((1, d)), _resident((1, d))],
        out_specs=pl.BlockSpec((tm, d), row),
        out_shape=jax.ShapeDtypeStruct((t_rows, d), F32),
        compiler_params=_cparams(("parallel",)),
        name="moe_combine_ln2",
    )(x1, pairs, pairs, route, mod, lw["ln2_g"], lw["ln2_b"])


def _moe_layer(x1, h2_tiles, route, mod, lw, geo, *, alpha):
    t_rows = x1.shape[0]
    mb = geo["mb"]
    n_assign = 2 * t_rows
    n_blocks = -(-(n_assign + N_EXPERTS * (mb - 1)) // mb)
    expert_of = route[:, 0:2].astype(jnp.int32).reshape(n_assign)
    onehot = (expert_of[:, None] == jnp.arange(N_EXPERTS, dtype=jnp.int32)[None, :]).astype(jnp.int32)
    csum = jnp.cumsum(onehot, axis=0)
    rank = jnp.take_along_axis(csum, expert_of[:, None], axis=1)[:, 0] - 1
    counts = csum[-1]
    padded = (counts + mb - 1) // mb * mb
    padded_end = jnp.cumsum(padded)
    dest = (padded_end - padded)[expert_of] + rank
    n_active = (padded_end[-1] // mb).astype(jnp.int32).reshape(1)
    block_expert = jnp.minimum(
        jnp.searchsorted(padded_end, jnp.arange(n_blocks, dtype=jnp.int32) * mb, side="right"),
        N_EXPERTS - 1).astype(jnp.int32)
    token_of_row = jnp.zeros((n_blocks * mb,), jnp.int32).at[dest].set(
        jnp.arange(n_assign, dtype=jnp.int32) // 2)
    rows = _gather_rows(h2_tiles, token_of_row, mb)
    out = _ffn_grouped(rows, lw["moe_wg"], lw["moe_wu"], lw["moe_wd"], block_expert, n_active, geo)
    pair_idx = jnp.concatenate([dest[0::2], dest[1::2]]).astype(jnp.int32)
    pairs = _gather_rows(out, pair_idx, geo["tm"])
    return _moe_ln2(x1, pairs, route, mod, lw, geo, alpha=alpha)


def _rope_tables(n_lat, tm):
    pos = jnp.arange(n_lat, dtype=jnp.int32)
    row = (pos // GRID_W).astype(F32)
    col = (pos % GRID_W).astype(F32)

    def axial(rot_dim):
        axis_dim = rot_dim // 2
        inv = ROPE_BASE ** (-jnp.arange(0, axis_dim, 2, dtype=F32) / axis_dim)
        ang_r = row[:, None] * inv[None, :]
        ang_c = col[:, None] * inv[None, :]
        ang = jnp.concatenate([ang_r, ang_r, ang_c, ang_c], axis=-1)
        return jnp.cos(ang), jnp.sin(ang)

    def pad(x, width, fill=0.0):
        return jnp.pad(x, ((0, 0), (0, width - x.shape[1])), constant_values=fill)

    cd, sd = axial(DA_HEAD_DIM)
    cm, sm = axial(MLA_ROPE)
    ones_nope = jnp.ones((n_lat, MLA_NOPE), F32)
    zeros_nope = jnp.zeros((n_lat, MLA_NOPE), F32)
    lat = jnp.concatenate([
        pad(jnp.concatenate([cd, cd], -1), LANES), pad(jnp.concatenate([sd, sd], -1), LANES),
        pad(jnp.concatenate([ones_nope, cm], -1), LANES), pad(jnp.concatenate([zeros_nope, sm], -1), LANES),
        pad(jnp.concatenate([cm, sm], -1), LANES)], axis=-1)
    ident_row = jnp.concatenate([
        pad(jnp.ones((1, 2 * DA_HEAD_DIM), F32), LANES), jnp.zeros((1, LANES), F32),
        pad(jnp.ones((1, MLA_NOPE + MLA_ROPE), F32), LANES), jnp.zeros((1, LANES), F32),
        pad(jnp.ones((1, MLA_ROPE), F32), LANES)], axis=-1)
    return jnp.concatenate([lat, jnp.broadcast_to(ident_row, (tm, 5 * LANES))], axis=0)


def _rot_cols(w, dim):
    lead = w.shape[:-1]
    g = w.reshape(lead + (-1, 2, dim // 2))
    return jnp.concatenate([-g[..., 1:2, :], g[..., 0:1, :]], axis=-2).reshape(w.shape)


def _to_slots(w, heads, width):
    lead = w.shape[:-1]
    g = w.reshape(lead + (heads, width))
    g = jnp.pad(g, [(0, 0)] * len(lead) + [(0, 0), (0, LANES - width)])
    return g.reshape(lead + (heads * LANES,))


def _layer_weights(l, p):
    d = p["w_in"].shape[1]
    w_in = p["w_in"][l]
    da = w_in[:, :3 * DA_Q]
    wq, wk, wv = da[:, :DA_Q], da[:, DA_Q:2 * DA_Q], da[:, 2 * DA_Q:]
    half = DA_HEAD_DIM // 2
    hy = w_in[:, 3 * DA_Q:3 * DA_Q + HY_IN]
    mla = w_in[:, 3 * DA_Q + HY_IN:]
    w_cq = mla[:, :MLA_Q_RANK]
    w_ckv = mla[:, MLA_Q_RANK:MLA_Q_RANK + MLA_KV_RANK]
    w_kr = mla[:, MLA_Q_RANK + MLA_KV_RANK:]
    kr_seg = jnp.pad(jnp.concatenate([w_kr, _rot_cols(w_kr, MLA_ROPE // 2)], -1),
                     ((0, 0), (0, LANES - 2 * MLA_ROPE)))
    w_in_ext = jnp.concatenate([
        _to_slots(wq, DA_HEADS, DA_V_DIM), _to_slots(_rot_cols(wq, half), DA_HEADS, DA_V_DIM),
        _to_slots(wk, DA_HEADS, DA_V_DIM), _to_slots(_rot_cols(wk, half), DA_HEADS, DA_V_DIM),
        _to_slots(wv, DA_HEADS, DA_V_DIM), hy, w_cq, w_ckv, kr_seg], axis=-1).astype(BF16)

    qh = MLA_NOPE + MLA_ROPE
    wuq = p["mla_w_uq"][l].reshape(MLA_Q_RANK, MLA_HEADS, qh)
    wuq_rot = jnp.concatenate([jnp.zeros_like(wuq[..., :MLA_NOPE]),
                               _rot_cols(wuq[..., MLA_NOPE:], MLA_ROPE // 2)], -1)
    wukv = p["mla_w_ukv"][l].reshape(MLA_KV_RANK, MLA_HEADS, MLA_NOPE + MLA_V)
    wuk = _to_slots(wukv[..., :MLA_NOPE].reshape(MLA_KV_RANK, -1), MLA_HEADS, MLA_NOPE)
    wuv = _to_slots(wukv[..., MLA_NOPE:].reshape(MLA_KV_RANK, -1), MLA_HEADS, MLA_V)
    lane_src = jnp.arange(LANES)[:, None]
    lane_dst = jnp.arange(MLA_SLOTS)[None, :] % LANES
    e2 = ((lane_src < 2 * MLA_ROPE) & (lane_dst == MLA_NOPE + lane_src % MLA_ROPE)).astype(BF16)

    slot_lane = jnp.arange(DA_SLOTS) % LANES
    vb_da = (slot_lane == DA_V_DIM).astype(F32)[None, :]
    vb_mla = ((jnp.arange(MLA_SLOTS) % LANES) == MLA_V).astype(F32)[None, :]

    w_out = p["w_out"][l]
    w_out_ext = jnp.concatenate([
        _to_slots(w_out[:DA_HEADS * DA_V_DIM].T, DA_HEADS, DA_V_DIM).T,
        w_out[DA_HEADS * DA_V_DIM:DA_HEADS * DA_V_DIM + HY_CH],
        _to_slots(w_out[DA_HEADS * DA_V_DIM + HY_CH:].T, MLA_HEADS, MLA_V).T], axis=0).astype(BF16)

    def pad_lanes(v):
        return jnp.pad(v, (0, LANES - v.shape[0]))[None, :]

    lam = jnp.concatenate([pad_lanes(p[k][l]) for k in
                           ("da_lambda_q1", "da_lambda_k1", "da_lambda_q2", "da_lambda_k2")], axis=0)
    hw = {
        "conv_w": p["hy_conv_w"][l][:, None, :], "conv_b": p["hy_conv_b"][l][None, :],
        "fw1": jnp.pad(p["hy_fw1"][l], ((0, HY_HIDDEN - HY_EMB), (0, 0))), "fb1": p["hy_fb1"][l][None, :],
        "fw2": p["hy_fw2"][l], "fb2": p["hy_fb2"][l][None, :],
        "fw3": p["hy_fw3"][l], "fb3": p["hy_fb3"][l][None, :],
        "fw4": p["hy_fw4"][l], "freq": p["hy_freq"][l][None, :], "bias": p["hy_bias"][l][:, None, :],
    }
    return {
        "w_in": w_in_ext, "qg": p["mla_q_norm_g"][l][None, :], "kvg": p["mla_kv_norm_g"][l][None, :],
        "wuq": _to_slots(wuq.reshape(MLA_Q_RANK, -1), MLA_HEADS, qh).astype(BF16),
        "wuqr": _to_slots(wuq_rot.reshape(MLA_Q_RANK, -1), MLA_HEADS, qh).astype(BF16),
        "wukv": jnp.concatenate([wuk, wuv], -1).astype(BF16), "e2": e2,
        "vb_da": vb_da, "vb_mla": vb_mla, "w_out": w_out_ext, "lam": lam,
        "subln_g": pad_lanes(p["da_subln_g"][l]), "hy": hw,
        "ln1_g": p["ln1_g"][l][None, :], "ln1_b": p["ln1_b"][l][None, :],
        "ln2_g": p["ln2_g"][l][None, :], "ln2_b": p["ln2_b"][l][None, :],
    }


def _hyena_consts(n):
    t = jnp.linspace(0.0, 1.0, n, dtype=F32)[:, None]
    bands = (HY_EMB - 1) // 2
    w = 2.0 * math.pi * jnp.arange(n, dtype=F32)[:, None] / n
    f = jnp.linspace(1e-4, bands - 1, bands, dtype=F32)[None, :]
    z = jnp.concatenate([t, jnp.cos(f * w), -jnp.sin(f * w)], axis=-1)
    z = jnp.pad(z, ((0, 0), (0, HY_HIDDEN - HY_EMB)))
    max_decay = math.log(HY_DECAY_TARGET) / HY_DECAY_SHORT_PCT
    min_decay = math.log(HY_DECAY_TARGET) / HY_DECAY_LONG_PCT
    deltas = jnp.linspace(min_decay, max_decay, HY_CH, dtype=F32)
    win = jnp.exp(-t * jnp.abs(deltas)[None, :]) + HY_DECAY_SHIFT
    k = jnp.arange(n, dtype=jnp.int32)
    ang = ((k[:, None] * k[None, :]) % (2 * n)).astype(F32) * (math.pi / n)
    sgn = (1 - 2 * (k % 2)).astype(F32)[:, None]
    return {"z": z, "win": win, "sgn": sgn, "c": jnp.cos(ang).astype(BF16), "s": jnp.sin(ang).astype(BF16)}


def _pick_tile(limit, *sizes):
    t = limit
    while any(s % t for s in sizes):
        t //= 2
    return t


def _make_geo(b, nl, nc):
    t_lat = b * nl
    tm = _pick_tile(512, nl, b * nc)
    tm_ffn = _pick_tile(512, nl, b * nc)
    geo = {"b": b, "nl": nl, "nc": nc, "tm": tm, "tm_ffn": tm_ffn, "fsub": 256,
           "tq": _pick_tile(512, nl), "mb": 512, "hps_da": 4, "hps_mla": 3}
    n_lat_t, per_b = t_lat // tm, nl // tm
    geo["mod_map"] = lambda i: (jnp.where(i < n_lat_t, i // per_b, b), 0, 0)
    geo["tab_map"] = lambda i: (jnp.where(i < n_lat_t, i % per_b, per_b), 0)
    n_lat_f, per_b_f = t_lat // tm_ffn, nl // tm_ffn
    geo["mod_map_ffn"] = lambda i: (jnp.where(i < n_lat_f, i // per_b_f, b), 0, 0)
    return geo


def kernel(x, c, ctx, c_ctx, ada_w, ada_b, w_in, da_lambda_q1, da_lambda_k1, da_lambda_q2, da_lambda_k2, da_subln_g, hy_conv_w, hy_conv_b, hy_fw1, hy_fb1, hy_fw2, hy_fb2, hy_fw3, hy_fb3, hy_fw4, hy_freq, hy_bias, mla_q_norm_g, mla_w_uq, mla_kv_norm_g, mla_w_ukv, w_out, ln1_g, ln1_b, ln2_g, ln2_b, ffn_w_gate, ffn_w_up, ffn_w_down, moe_router, moe_w_gate, moe_w_up, moe_w_down):
    p = dict(w_in=w_in, da_lambda_q1=da_lambda_q1, da_lambda_k1=da_lambda_k1, da_lambda_q2=da_lambda_q2,
             da_lambda_k2=da_lambda_k2, da_subln_g=da_subln_g, hy_conv_w=hy_conv_w, hy_conv_b=hy_conv_b,
             hy_fw1=hy_fw1, hy_fb1=hy_fb1, hy_fw2=hy_fw2, hy_fb2=hy_fb2, hy_fw3=hy_fw3, hy_fb3=hy_fb3,
             hy_fw4=hy_fw4, hy_freq=hy_freq, hy_bias=hy_bias, mla_q_norm_g=mla_q_norm_g, mla_w_uq=mla_w_uq,
             mla_kv_norm_g=mla_kv_norm_g, mla_w_ukv=mla_w_ukv, w_out=w_out, ln1_g=ln1_g, ln1_b=ln1_b,
             ln2_g=ln2_g, ln2_b=ln2_b)
    b, nl, d = x.shape
    nc = ctx.shape[1]
    depth = ada_w.shape[0]
    assert nl % nc == 0 and nl % GRID_W == 0 and nc % 8 == 0
    alpha = (2 * depth) ** 0.25
    t_lat, t_all = b * nl, b * (nl + nc)

    geo = _make_geo(b, nl, nc)
    tm = geo["tm"]

    bp = -(-(b + 1) // 8) * 8
    c_all = jnp.zeros((bp, d), F32).at[:b].set(c).at[b].set(c_ctx)
    mod_all = _ada_all(c_all, ada_w, ada_b).reshape(depth, bp, 6, d)

    tab = _rope_tables(nl, tm)
    consts_lat = _hyena_consts(nl)
    consts_ctx = _hyena_consts(nc)
    xs = jnp.concatenate([x.reshape(t_lat, d), ctx.reshape(b * nc, d)], axis=0)

    for l in range(depth):
        need_ctx = l < depth - 1
        lw = _layer_weights(l, p)
        mod = mod_all[l]
        lambda_init = 0.8 - 0.6 * math.exp(-0.3 * l)

        daq, dak, dav, hy_p, mlaq, mlak, mlav = _in_proj(xs, mod, lw, tab, geo)
        da_kw = dict(heads=DA_HEADS, hps=geo["hps_da"], n_maps=2, dv=DA_V_DIM, post_scale=1.0 - lambda_init,
                     lam=lw["lam"], gain=lw["subln_g"])
        mla_kw = dict(heads=MLA_HEADS, hps=geo["hps_mla"], n_maps=1, dv=MLA_V)
        mix_lat = (_attention(daq, dak, dav, geo, lat_queries=True, **da_kw),
                   _hy_conv(hy_p, lw["hy"], _hy_filter(lw["hy"], consts_lat), consts_lat, geo, ctx=False),
                   _attention(mlaq, mlak, mlav, geo, lat_queries=True, **mla_kw))
        mix_ctx = None
        if need_ctx:
            mix_ctx = (_attention(daq, dak, dav, geo, lat_queries=False, **da_kw),
                       _hy_conv(hy_p, lw["hy"], _hy_filter(lw["hy"], consts_ctx), consts_ctx, geo, ctx=True),
                       _attention(mlaq, mlak, mlav, geo, lat_queries=False, **mla_kw))

        idx = l // 2
        if l % 2 == 0:
            x1, h2 = _out_proj(mix_lat, mix_ctx, xs, mod, lw, geo, alpha=alpha)
            xs = _ffn_dense(h2, ffn_w_gate[idx:idx + 1].astype(BF16), ffn_w_up[idx:idx + 1].astype(BF16),
                            ffn_w_down[idx:idx + 1].astype(BF16), x1, mod, lw, geo, alpha=alpha)
        else:
            router = jnp.pad(moe_router[idx], ((0, 0), (0, LANES - N_EXPERTS))).astype(BF16)
            x1, h2, route = _out_proj(mix_lat, mix_ctx, xs, mod, lw, geo, alpha=alpha, router=router)
            lw["moe_wg"] = moe_w_gate[idx].astype(BF16)
            lw["moe_wu"] = moe_w_up[idx].astype(BF16)
            lw["moe_wd"] = moe_w_down[idx].astype(BF16)
            xs = _moe_layer(x1, h2, route, mod, lw, geo, alpha=alpha)
    return xs[:t_lat].reshape(b, nl, d)
```

```python
import functools
import math

import jax
import jax.numpy as jnp
from jax import lax
from jax.experimental import pallas as pl
from jax.experimental.pallas import tpu as pltpu

F32 = jnp.float32
BF16 = jnp.bfloat16

GRID_W = 64
DA_HEADS = 4
DA_HEAD_DIM = 48
DA_V_DIM = 2 * DA_HEAD_DIM
HY_CH = 256
HY_ORDER = 2
HY_EMB = 33
HY_HIDDEN = 64
HY_DECAY_TARGET = 1e-2
HY_DECAY_SHORT_PCT = 0.3
HY_DECAY_LONG_PCT = 1.5
HY_DECAY_SHIFT = 0.05
MLA_HEADS = 6
MLA_Q_RANK = 256
MLA_KV_RANK = 128
MLA_NOPE = 64
MLA_ROPE = 32
MLA_V = 64
N_EXPERTS = 8
ROPE_BASE = 10000.0
LN_EPS = 1e-5
RMS_EPS = 1e-6

LANES = 128
TOKEN_TILE = 8
VMEM_LIMIT = 56 * 1024 * 1024

DA_Q = DA_HEADS * 2 * DA_HEAD_DIM
DA_SLOTS = DA_HEADS * LANES
MLA_SLOTS = MLA_HEADS * LANES
HY_IN = (HY_ORDER + 1) * HY_CH
MIX_EXT = DA_SLOTS + HY_CH + MLA_SLOTS

_C_Q, _C_K, _C_V = 0, 512, 1024
_C_HY = 1536
_C_CQ = _C_HY + HY_IN
_C_CKV = _C_CQ + MLA_Q_RANK
_C_KROPE = _C_CKV + MLA_KV_RANK
IN_EXT = _C_KROPE + LANES


def _cparams(sem):
    return pltpu.CompilerParams(dimension_semantics=sem, vmem_limit_bytes=VMEM_LIMIT)


def _resident(shape):
    nd = len(shape)
    return pl.BlockSpec(shape, lambda *_: (0,) * nd, pipeline_mode=pl.Buffered(1))


def _dot(a, b):
    return jnp.dot(a, b, preferred_element_type=F32)


def _dot_nt(a, b):
    return lax.dot_general(a, b, (((1,), (1,)), ((), ())), preferred_element_type=F32)


def _layer_norm_rows(y, g, b):
    mu = jnp.mean(y, axis=-1, keepdims=True)
    yc = y - mu
    var = jnp.mean(yc * yc, axis=-1, keepdims=True)
    return yc * lax.rsqrt(var + LN_EPS) * g + b


def _sigmoid(x):
    return 1.0 / (1.0 + jnp.exp(-x))


def _ada_kernel(c_ref, w_ref, b_ref, o_ref):
    c = c_ref[...]
    act = (c * _sigmoid(c)).astype(BF16)
    o_ref[0] = _dot(act, w_ref[0].astype(BF16)) + b_ref[0]


def _ada_all(c_all, ada_w, ada_b):
    depth, d, n6 = ada_w.shape
    bp = c_all.shape[0]
    tn = 512
    return pl.pallas_call(
        _ada_kernel,
        grid=(depth, n6 // tn),
        in_specs=[
            pl.BlockSpec((bp, d), lambda l, j: (0, 0)),
            pl.BlockSpec((1, d, tn), lambda l, j: (l, 0, j)),
            pl.BlockSpec((1, 1, tn), lambda l, j: (l, 0, j)),
        ],
        out_specs=pl.BlockSpec((1, bp, tn), lambda l, j: (l, 0, j)),
        out_shape=jax.ShapeDtypeStruct((depth, bp, n6), F32),
        compiler_params=_cparams(("parallel", "parallel")),
        name="ada_mod",
    )(c_all, ada_w, ada_b.reshape(depth, 1, n6))


def _tile_lanes(x, reps):
    return jnp.concatenate([x] * reps, axis=-1)


def _rms_rows(x, g):
    return x * lax.rsqrt(jnp.mean(x * x, axis=-1, keepdims=True) + RMS_EPS) * g


def _rotate_half_slots(x, slots, half, first_half):
    parts = []
    for h in range(slots):
        xs = x[:, h * LANES:(h + 1) * LANES]
        parts.append(jnp.where(first_half, pltpu.roll(xs, LANES - half, 1), pltpu.roll(xs, half, 1)))
    return jnp.concatenate(parts, axis=-1)


def _in_proj_kernel(x_ref, mod_ref, w_ref, tab_ref, qg_ref, kvg_ref, wuq_ref, wukv_ref,
                    e2_ref, vbda_ref, vbmla_ref,
                    daq_ref, dak_ref, dav_ref, hy_ref, mlaq_ref, mlak_ref, mlav_ref,
                    *, da_scale, mla_scale):
    m = mod_ref[0]
    h = (x_ref[...] * (1.0 + m[1:2]) + m[0:1]).astype(BF16)

    def seg(a, b):
        return _dot(h, w_ref[:, a:b])

    lane = lax.broadcasted_iota(jnp.int32, (h.shape[0], LANES), 1)
    da_half = DA_HEAD_DIM // 4
    da_first = (lane % (2 * da_half)) < da_half
    cos_da = _tile_lanes(tab_ref[:, 0:128], DA_HEADS)
    sin_da = _tile_lanes(tab_ref[:, 128:256], DA_HEADS)
    q = seg(_C_Q, _C_K)
    daq_ref[...] = ((q * cos_da + _rotate_half_slots(q, DA_HEADS, da_half, da_first) * sin_da)
                    * da_scale).astype(BF16)
    k = seg(_C_K, _C_V)
    dak_ref[...] = (k * cos_da + _rotate_half_slots(k, DA_HEADS, da_half, da_first) * sin_da).astype(BF16)
    dav_ref[...] = (seg(_C_V, _C_HY) + vbda_ref[...]).astype(BF16)
    hy_ref[...] = seg(_C_HY, _C_CQ)

    cqn = _rms_rows(seg(_C_CQ, _C_CKV), qg_ref[...]).astype(BF16)
    mla_half = MLA_ROPE // 4
    mla_first = ((lane - MLA_NOPE) % (2 * mla_half)) < mla_half
    cos_mq = _tile_lanes(tab_ref[:, 256:384], MLA_HEADS)
    sin_mq = _tile_lanes(tab_ref[:, 384:512], MLA_HEADS)
    mq = _dot(cqn, wuq_ref[...])
    mlaq_ref[...] = ((mq * cos_mq + _rotate_half_slots(mq, MLA_HEADS, mla_half, mla_first) * sin_mq)
                     * mla_scale).astype(BF16)

    ckvn = _rms_rows(seg(_C_CKV, _C_KROPE), kvg_ref[...]).astype(BF16)
    kv = _dot(ckvn, wukv_ref[...])
    krw = seg(_C_KROPE, IN_EXT) * tab_ref[:, 512:640]
    kr_hi = krw.astype(BF16)
    kr_lo = (krw - kr_hi.astype(F32)).astype(BF16)
    placed = _dot(kr_hi, e2_ref[...]) + _dot(kr_lo, e2_ref[...])
    mlak_ref[...] = (kv[:, :MLA_SLOTS] + placed).astype(BF16)
    mlav_ref[...] = (kv[:, MLA_SLOTS:] + vbmla_ref[...]).astype(BF16)


def _in_proj(xs, mod, lw, tab, geo):
    t_rows, d = xs.shape
    tm = geo["tm"]
    row = lambda i: (i, 0)
    outs = [(DA_SLOTS, BF16), (DA_SLOTS, BF16), (DA_SLOTS, BF16), (HY_IN, F32),
            (MLA_SLOTS, BF16), (MLA_SLOTS, BF16), (MLA_SLOTS, BF16)]
    kern = functools.partial(_in_proj_kernel, da_scale=DA_HEAD_DIM ** -0.5,
                             mla_scale=(MLA_NOPE + MLA_ROPE) ** -0.5)
    return pl.pallas_call(
        kern,
        grid=(t_rows // tm,),
        in_specs=[
            pl.BlockSpec((tm, d), row),
            pl.BlockSpec((1, 6, d), geo["mod_map"]),
            _resident((d, IN_EXT)),
            pl.BlockSpec((tm, 5 * LANES), geo["tab_map"]),
            _resident((1, MLA_Q_RANK)),
            _resident((1, MLA_KV_RANK)),
            _resident((MLA_Q_RANK, MLA_SLOTS)),
            _resident((MLA_KV_RANK, 2 * MLA_SLOTS)),
            _resident((LANES, MLA_SLOTS)),
            _resident((1, DA_SLOTS)),
            _resident((1, MLA_SLOTS)),
        ],
        out_specs=[pl.BlockSpec((tm, w), row) for w, _ in outs],
        out_shape=[jax.ShapeDtypeStruct((t_rows, w), dt) for w, dt in outs],
        compiler_params=_cparams(("parallel",)),
        name="in_proj",
    )(xs, mod, lw["w_in"], tab, lw["qg"], lw["kvg"], lw["wuq"], lw["wukv"],
      lw["e2"], lw["vb_da"], lw["vb_mla"])


def _attn_kernel(*refs, n_maps, dv, post_scale, lat_queries, hps):
    if n_maps == 2:
        lam_ref, g_ref, q_ref, kl_ref, kc_ref, vl_ref, vc_ref, o_ref = refs
    else:
        q_ref, kl_ref, kc_ref, vl_ref, vc_ref, o_ref = refs
    tq = q_ref.shape[0]
    lane = lax.broadcasted_iota(jnp.int32, (tq, LANES), 1)

    def attend(qm, hs, include_lat):
        sc = _dot_nt(qm, kc_ref[:, hs])
        mx = jnp.max(sc, axis=-1, keepdims=True)
        if include_lat:
            sl = _dot_nt(qm, kl_ref[:, hs])
            mx = jnp.maximum(mx, jnp.max(sl, axis=-1, keepdims=True))
            ol = _dot(jnp.exp(sl - mx).astype(BF16), vl_ref[:, hs])
        ol_c = _dot(jnp.exp(sc - mx).astype(BF16), vc_ref[:, hs])
        ol = ol + ol_c if include_lat else ol_c
        den = jnp.sum(jnp.where(lane == dv, ol, 0.0), axis=-1, keepdims=True)
        return ol / den

    def head(h, include_lat):
        hs = slice(h * LANES, (h + 1) * LANES)
        q = q_ref[:, hs]
        if n_maps == 2:
            q1 = jnp.where(lane < DA_HEAD_DIM, q, jnp.zeros_like(q))
            q2 = jnp.where(lane < DA_HEAD_DIM, jnp.zeros_like(q), q)
            lp = lam_ref[...]
            lam = (jnp.exp(jnp.sum(lp[0:1] * lp[1:2], axis=-1, keepdims=True))
                   - jnp.exp(jnp.sum(lp[2:3] * lp[3:4], axis=-1, keepdims=True))
                   + (1.0 - post_scale))
            o = attend(q1, hs, include_lat) - lam * attend(q2, hs, include_lat)
            o = jnp.where(lane < dv, o, 0.0)
            ms = jnp.sum(o * o, axis=-1, keepdims=True) * (1.0 / dv)
            return o * lax.rsqrt(ms + RMS_EPS) * g_ref[...] * post_scale
        return jnp.where(lane < dv, attend(q, hs, include_lat), 0.0)

    o = jnp.concatenate([head(h, lat_queries) for h in range(hps)], axis=-1)
    o_ref[...] = o.astype(o_ref.dtype)


def _attention(q, k, v, geo, *, heads, hps, n_maps, dv, lat_queries, post_scale=1.0, lam=None, gain=None):
    b, nl, nc = geo["b"], geo["nl"], geo["nc"]
    tq = geo["tq"] if lat_queries else nc
    n_q_tiles = (nl if lat_queries else nc) // tq
    q_blk0 = 0 if lat_queries else b * nl // nc
    ctx_kv0 = b * nl // nc
    width = hps * LANES
    kv_lat = pl.BlockSpec((nl if lat_queries else 8, width), lambda bi, hi, j: (bi if lat_queries else 0, hi))
    kv_ctx = pl.BlockSpec((nc, width), lambda bi, hi, j: (ctx_kv0 + bi, hi))
    in_specs = [pl.BlockSpec((tq, width), lambda bi, hi, j: (q_blk0 + bi * n_q_tiles + j, hi)),
                kv_lat, kv_ctx, kv_lat, kv_ctx]
    args = [q, k, k, v, v]
    if n_maps == 2:
        in_specs = [pl.BlockSpec((4, LANES), lambda bi, hi, j: (0, 0)),
                    pl.BlockSpec((1, LANES), lambda bi, hi, j: (0, 0))] + in_specs
        args = [lam, gain] + args
    kern = functools.partial(_attn_kernel, n_maps=n_maps, dv=dv, post_scale=post_scale,
                             lat_queries=lat_queries, hps=hps)
    return pl.pallas_call(
        kern,
        grid=(b, heads // hps, n_q_tiles),
        in_specs=in_specs,
        out_specs=pl.BlockSpec((tq, width), lambda bi, hi, j: (bi * n_q_tiles + j, hi)),
        out_shape=jax.ShapeDtypeStruct((b * n_q_tiles * tq, q.shape[1]), BF16),
        compiler_params=_cparams(("parallel", "parallel", "arbitrary")),
        name=("diff_attn" if n_maps == 2 else "mla_attn") + ("_lat" if lat_queries else "_ctx"),
    )(*args)


def _hy_filter_kernel(z_ref, w1_ref, b1_ref, w2_ref, b2_ref, w3_ref, b3_ref, w4_ref, fr_ref,
                      win_ref, sgn_ref, a_ref, d_ref, hn_ref, *, rows):
    hp = lax.Precision.HIGHEST
    fr = fr_ref[...]

    def lin(x, w_ref, b_ref):
        return jnp.dot(x, w_ref[...], precision=hp, preferred_element_type=F32) + b_ref[...]

    hdn = jnp.sin(fr * lin(z_ref[...], w1_ref, b1_ref))
    hdn = jnp.sin(fr * lin(hdn, w2_ref, b2_ref))
    hdn = jnp.sin(fr * lin(hdn, w3_ref, b3_ref))
    filt = jnp.dot(hdn, w4_ref[...], precision=hp, preferred_element_type=F32)
    win = _tile_lanes(win_ref[...], HY_ORDER)
    half = HY_ORDER * HY_CH
    fwd = filt[:, :half] * win
    bwd = filt[:, half:] * win
    i = pl.program_id(0)
    row = lax.broadcasted_iota(jnp.int32, bwd.shape, 0) + i * rows
    bwd = jnp.where(row == 0, 0.0, bwd)
    a = fwd + bwd
    a_ref[...] = a.astype(BF16)
    d_ref[...] = (bwd - fwd).astype(BF16)

    @pl.when(i == 0)
    def _():
        hn_ref[...] = jnp.zeros_like(hn_ref)

    hn_ref[...] += jnp.sum(a * sgn_ref[...], axis=0, keepdims=True)


def _hy_spectrum_kernel(c_ref, s_ref, a_ref, d_ref, hr_ref, hi_ref):
    hr_ref[...] = _dot(c_ref[...], a_ref[...])
    hi_ref[...] = _dot(s_ref[...], d_ref[...])


def _hy_filter(hw, consts):
    n = consts["c"].shape[0]
    half = HY_ORDER * HY_CH
    tr = _pick_tile(512, n)
    row = lambda i: (i, 0)
    weights = (hw["fw1"], hw["fb1"], hw["fw2"], hw["fb2"], hw["fw3"], hw["fb3"], hw["fw4"], hw["freq"])
    a, d, hn = pl.pallas_call(
        functools.partial(_hy_filter_kernel, rows=tr),
        grid=(n // tr,),
        in_specs=([pl.BlockSpec((tr, HY_HIDDEN), row)] + [_resident(w.shape) for w in weights]
                  + [pl.BlockSpec((tr, HY_CH), row), pl.BlockSpec((tr, 1), row)]),
        out_specs=[pl.BlockSpec((tr, half), row), pl.BlockSpec((tr, half), row),
                   pl.BlockSpec((1, half), lambda i: (0, 0))],
        out_shape=[jax.ShapeDtypeStruct((n, half), BF16), jax.ShapeDtypeStruct((n, half), BF16),
                   jax.ShapeDtypeStruct((1, half), F32)],
        compiler_params=_cparams(("arbitrary",)),
        name="hyena_filter",
    )(consts["z"], *weights, consts["win"], consts["sgn"])
    col = lambda j: (0, j)
    hr, hi = pl.pallas_call(
        _hy_spectrum_kernel,
        grid=(half // HY_CH,),
        in_specs=[_resident((n, n)), _resident((n, n)), pl.BlockSpec((n, HY_CH), col),
                  pl.BlockSpec((n, HY_CH), col)],
        out_specs=[pl.BlockSpec((n, HY_CH), col), pl.BlockSpec((n, HY_CH), col)],
        out_shape=[jax.ShapeDtypeStruct((n, half), F32), jax.ShapeDtypeStruct((n, half), F32)],
        compiler_params=_cparams(("parallel",)),
        name="hyena_spectrum",
    )(consts["c"], consts["s"], a, d)
    return hr, hi, hn


def _hy_conv_kernel(*refs, n, kc, rc):
    (p_ref, cw_ref, cb_ref, c_ref, s_ref, hr_ref, hi_ref, hn_ref, sgn_ref, bias_ref, o_ref,
     z_s, zb_s, y_s) = refs
    ri = lax.broadcasted_iota(jnp.int32, (rc, HY_CH), 0)
    zero_row = jnp.zeros((1, HY_CH), F32)

    def short_conv(g, r0):
        cs = slice(g * HY_CH, (g + 1) * HY_CH)
        p = p_ref[r0:r0 + rc, cs]
        up = p_ref[r0 - 1:r0, cs] if r0 > 0 else zero_row
        dn = p_ref[r0 + rc:r0 + rc + 1, cs] if r0 + rc < n else zero_row
        prev = jnp.where(ri == 0, up, pltpu.roll(p, 1, 0))
        nxt = jnp.where(ri == rc - 1, dn, pltpu.roll(p, rc - 1, 0))
        return prev * cw_ref[0, :, cs] + p * cw_ref[1, :, cs] + nxt * cw_ref[2, :, cs] + cb_ref[:, cs]

    for r0 in range(0, n, rc):
        z_s[r0:r0 + rc, :] = short_conv(HY_ORDER, r0)
    for i in range(HY_ORDER):
        cs = slice(i * HY_CH, (i + 1) * HY_CH)
        zb_s[...] = z_s[...].astype(BF16)
        xn = jnp.sum(z_s[...] * sgn_ref[...], axis=0, keepdims=True)
        y_s[...] = sgn_ref[...] * (xn * hn_ref[:, cs] * (0.5 / n))
        for k0 in range(0, n, kc):
            ks = slice(k0, k0 + kc)
            zb = zb_s[...]
            a = _dot(c_ref[ks, :], zb)
            bm = _dot(s_ref[ks, :], zb)
            hr = hr_ref[ks, cs]
            hi = hi_ref[ks, cs]
            yr = a * hr + bm * hi
            if k0 == 0:
                yr = jnp.where(lax.broadcasted_iota(jnp.int32, yr.shape, 0) == 0, 0.5 * yr, yr)
            yi = a * hi - bm * hr
            y_s[...] += (_dot(c_ref[:, ks], yr.astype(BF16)) - _dot(s_ref[:, ks], yi.astype(BF16))) * (1.0 / n)
        dst = o_ref if i == HY_ORDER - 1 else z_s
        for r0 in range(0, n, rc):
            rs = slice(r0, r0 + rc)
            dst[rs, :] = short_conv(i, r0) * (y_s[rs, :] + z_s[rs, :] * bias_ref[i])


def _hy_conv(hy_p, hw, spec, consts, geo, *, ctx):
    b, nl, nc = geo["b"], geo["nl"], geo["nc"]
    n = nc if ctx else nl
    blk0 = b * nl // nc if ctx else 0
    hr, hi, hn = spec
    args = [hy_p, hw["conv_w"], hw["conv_b"], consts["c"], consts["s"], hr, hi, hn, consts["sgn"],
            hw["bias"]]
    in_specs = ([pl.BlockSpec((n, HY_IN), lambda bi: (blk0 + bi, 0), pipeline_mode=pl.Buffered(1))]
                + [_resident(a.shape) for a in args[1:]])
    kern = functools.partial(_hy_conv_kernel, n=n, kc=_pick_tile(512, n), rc=_pick_tile(256, n))
    return pl.pallas_call(
        kern,
        grid=(b,),
        in_specs=in_specs,
        out_specs=pl.BlockSpec((n, HY_CH), lambda bi: (bi, 0)),
        out_shape=jax.ShapeDtypeStruct((b * n, HY_CH), F32),
        scratch_shapes=[pltpu.VMEM((n, HY_CH), F32), pltpu.VMEM((n, HY_CH), BF16), pltpu.VMEM((n, HY_CH), F32)],
        compiler_params=_cparams(("parallel",)),
        name="hyena_conv_ctx" if ctx else "hyena_conv_lat",
    )(*args)


def _top2_route(logits):
    lane = lax.broadcasted_iota(jnp.int32, logits.shape, 1)
    neg = jnp.float32(-jnp.inf)
    lg = jnp.where(lane < N_EXPERTS, logits, neg)
    m1 = jnp.max(lg, axis=-1, keepdims=True)
    i1 = jnp.min(jnp.where(lg == m1, lane, LANES), axis=-1, keepdims=True)
    lg2 = jnp.where(lane == i1, neg, lg)
    m2 = jnp.max(lg2, axis=-1, keepdims=True)
    i2 = jnp.min(jnp.where(lg2 == m2, lane, LANES), axis=-1, keepdims=True)
    e = jnp.exp(m2 - m1)
    g1 = 1.0 / (1.0 + e)
    g2 = e * g1
    out = jnp.where(lane == 0, i1.astype(F32), 0.0)
    out = jnp.where(lane == 1, i2.astype(F32), out)
    out = jnp.where(lane == 2, g1, out)
    return jnp.where(lane == 3, g2, out)


def _out_proj_kernel(*refs, alpha, moe, n_lat_tiles, has_ctx):
    n_mix = 6 if has_ctx else 3
    mix_refs, refs = refs[:n_mix], refs[n_mix:]
    if moe:
        w_ref, x_ref, mod_ref, g_ref, b_ref, rw_ref, x1_ref, h2_ref, route_ref = refs[:9]
    else:
        w_ref, x_ref, mod_ref, g_ref, b_ref, x1_ref, h2_ref = refs[:7]

    def mix(da_ref, hy_ref, mla_ref):
        return (_dot(da_ref[...], w_ref[0:DA_SLOTS])
                + _dot(hy_ref[...].astype(BF16), w_ref[DA_SLOTS:DA_SLOTS + HY_CH])
                + _dot(mla_ref[...], w_ref[DA_SLOTS + HY_CH:MIX_EXT]))

    if has_ctx:
        o_ref = refs[-1]
        is_lat = pl.program_id(0) < n_lat_tiles

        @pl.when(is_lat)
        def _():
            o_ref[...] = mix(*mix_refs[:3])

        @pl.when(jnp.logical_not(is_lat))
        def _():
            o_ref[...] = mix(*mix_refs[3:])

        o = o_ref[...]
    else:
        o = mix(*mix_refs)
    m = mod_ref[0]
    x1 = _layer_norm_rows(alpha * x_ref[...] + m[2:3] * o, g_ref[...], b_ref[...])
    x1_ref[...] = x1
    h2 = x1 * (1.0 + m[4:5]) + m[3:4]
    if moe:
        _to_token_tiles(h2_ref, h2)
        route_ref[...] = _top2_route(_dot(h2.astype(BF16), rw_ref[...]))
    else:
        h2_ref[...] = h2.astype(BF16)


def _out_proj(mix_lat, mix_ctx, xs, mod, lw, geo, *, alpha, router=None):
    d = xs.shape[1]
    tm = geo["tm"]
    row = lambda i: (i, 0)
    moe = router is not None
    has_ctx = mix_ctx is not None
    n_lat_tiles = mix_lat[0].shape[0] // tm
    t_rows = mix_lat[0].shape[0] + (mix_ctx[0].shape[0] if has_ctx else 0)
    widths = (DA_SLOTS, HY_CH, MLA_SLOTS)
    lat_row = lambda i: (jnp.minimum(i, n_lat_tiles - 1), 0)
    ctx_row = lambda i: (jnp.maximum(i - n_lat_tiles, 0), 0)
    in_specs = [pl.BlockSpec((tm, w), lat_row) for w in widths]
    args = list(mix_lat)
    if has_ctx:
        in_specs += [pl.BlockSpec((tm, w), ctx_row) for w in widths]
        args += list(mix_ctx)
    in_specs += [_resident((MIX_EXT, d)), pl.BlockSpec((tm, d), row),
                 pl.BlockSpec((1, 6, d), geo["mod_map"]), _resident((1, d)), _resident((1, d))]
    args += [lw["w_out"], xs, mod, lw["ln1_g"], lw["ln1_b"]]
    if moe:
        h2_spec = pl.BlockSpec((tm * TOKEN_TILE, LANES), row)
        h2_shape = jax.ShapeDtypeStruct((t_rows * TOKEN_TILE, LANES), F32)
    else:
        h2_spec, h2_shape = pl.BlockSpec((tm, d), row), jax.ShapeDtypeStruct((t_rows, d), BF16)
    out_specs = [pl.BlockSpec((tm, d), row), h2_spec]
    out_shape = [jax.ShapeDtypeStruct((t_rows, d), F32), h2_shape]
    if moe:
        in_specs.append(_resident((d, LANES)))
        args.append(router)
        out_specs.append(pl.BlockSpec((tm, LANES), row))
        out_shape.append(jax.ShapeDtypeStruct((t_rows, LANES), F32))
    return pl.pallas_call(
        functools.partial(_out_proj_kernel, alpha=alpha, moe=moe, n_lat_tiles=n_lat_tiles, has_ctx=has_ctx),
        grid=(t_rows // tm,),
        in_specs=in_specs,
        out_specs=out_specs,
        out_shape=out_shape,
        scratch_shapes=[pltpu.VMEM((tm, d), F32)] if has_ctx else [],
        compiler_params=_cparams(("parallel",)),
        name="out_proj_ln1",
    )(*args)


def _from_token_tiles(ref, rows):
    return jnp.concatenate([ref[pl.ds(c, rows, stride=TOKEN_TILE), :] for c in range(TOKEN_TILE)], axis=-1)


def _to_token_tiles(ref, val):
    rows = val.shape[0]
    for c in range(TOKEN_TILE):
        ref[pl.ds(c, rows, stride=TOKEN_TILE), :] = val[:, c * LANES:(c + 1) * LANES]


def _swiglu_kernel(*refs, alpha, grouped, fsub):
    if grouped:
        be_ref, na_ref, h_ref, wg_ref, wu_ref, wd_ref, o_ref, act_ref = refs
    else:
        h_ref, wg_ref, wu_ref, wd_ref, x_ref, mod_ref, g_ref, b_ref, o_ref, act_ref = refs

    def body():
        if grouped:
            h = _from_token_tiles(h_ref, act_ref.shape[0]).astype(BF16)
        else:
            h = h_ref[...]
        for c0 in range(0, wg_ref.shape[2], fsub):
            cs = slice(c0, c0 + fsub)
            g = _dot(h, wg_ref[0, :, cs])
            u = _dot(h, wu_ref[0, :, cs])
            act_ref[:, cs] = (g * _sigmoid(g) * u).astype(BF16)
        y = _dot(act_ref[...], wd_ref[0])
        if grouped:
            _to_token_tiles(o_ref, y)
        else:
            m = mod_ref[0]
            o_ref[...] = _layer_norm_rows(alpha * x_ref[...] + m[5:6] * y, g_ref[...], b_ref[...])

    if grouped:
        active = pl.program_id(0) < na_ref[0]
        pl.when(active)(body)

        @pl.when(jnp.logical_not(active))
        def _():
            o_ref[...] = jnp.zeros_like(o_ref)
    else:
        body()


def _ffn_dense(h2, wg, wu, wd, x1, mod, lw, geo, *, alpha):
    t_rows, d = h2.shape
    f_dim = wg.shape[-1]
    tm = geo["tm_ffn"]
    row = lambda i: (i, 0)
    return pl.pallas_call(
        functools.partial(_swiglu_kernel, alpha=alpha, grouped=False, fsub=geo["fsub"]),
        grid=(t_rows // tm,),
        in_specs=[pl.BlockSpec((tm, d), row), _resident((1, d, f_dim)), _resident((1, d, f_dim)),
                  _resident((1, f_dim, d)), pl.BlockSpec((tm, d), row),
                  pl.BlockSpec((1, 6, d), geo["mod_map_ffn"]), _resident((1, d)), _resident((1, d))],
        out_specs=pl.BlockSpec((tm, d), row),
        out_shape=jax.ShapeDtypeStruct((t_rows, d), F32),
        scratch_shapes=[pltpu.VMEM((tm, f_dim), BF16)],
        compiler_params=_cparams(("parallel",)),
        name="ffn_swiglu_ln2",
    )(h2, wg, wu, wd, x1, mod, lw["ln2_g"], lw["ln2_b"])


def _ffn_grouped(rows, wg, wu, wd, block_expert, n_active, geo):
    d, f_dim = wg.shape[1], wg.shape[2]
    mb = geo["mb"]
    r_rows = rows.shape[0] // TOKEN_TILE
    blk = lambda i, be, na: (jnp.minimum(i, na[0] - 1), 0)
    wmap = lambda i, be, na: (be[jnp.minimum(i, na[0] - 1)], 0, 0)
    grid_spec = pltpu.PrefetchScalarGridSpec(
        num_scalar_prefetch=2,
        grid=(r_rows // mb,),
        in_specs=[pl.BlockSpec((mb * TOKEN_TILE, LANES), blk),
                  pl.BlockSpec((1, d, f_dim), wmap), pl.BlockSpec((1, d, f_dim), wmap),
                  pl.BlockSpec((1, f_dim, d), wmap)],
        out_specs=pl.BlockSpec((mb * TOKEN_TILE, LANES), lambda i, be, na: (i, 0)),
        scratch_shapes=[pltpu.VMEM((mb, f_dim), BF16)],
    )
    return pl.pallas_call(
        functools.partial(_swiglu_kernel, alpha=0.0, grouped=True, fsub=geo["fsub"]),
        grid_spec=grid_spec,
        out_shape=jax.ShapeDtypeStruct(rows.shape, F32),
        compiler_params=_cparams(("arbitrary",)),
        name="moe_experts",
    )(block_expert, n_active, rows, wg, wu, wd)


def _gather_kernel(idx_ref, src_ref, o_ref, sem, *, rows):
    group = 8

    def start(g, carry):
        for j in range(group):
            r = g * group + j
            src = pl.multiple_of(idx_ref[r] * TOKEN_TILE, TOKEN_TILE)
            dst = pl.multiple_of(r * TOKEN_TILE, TOKEN_TILE)
            pltpu.make_async_copy(src_ref.at[pl.ds(src, TOKEN_TILE)], o_ref.at[pl.ds(dst, TOKEN_TILE)],
                                  sem).start(priority=j % 2)
        return carry

    lax.fori_loop(0, rows // group, start, 0)
    pltpu.make_async_copy(o_ref, o_ref, sem).wait()


def _gather_rows(src, idx, rows_per_step):
    m = idx.shape[0]
    return pl.pallas_call(
        functools.partial(_gather_kernel, rows=rows_per_step),
        grid=(m // rows_per_step,),
        in_specs=[pl.BlockSpec((rows_per_step,), lambda i: (i,), memory_space=pltpu.SMEM),
                  pl.BlockSpec(memory_space=pl.ANY)],
        out_specs=pl.BlockSpec((rows_per_step * TOKEN_TILE, LANES), lambda i: (i, 0)),
        out_shape=jax.ShapeDtypeStruct((m * TOKEN_TILE, LANES), src.dtype),
        scratch_shapes=[pltpu.SemaphoreType.DMA(())],
        compiler_params=_cparams(("arbitrary",)),
        name="gather_rows",
    )(idx, src)


def _moe_ln2_kernel(x_ref, y1_ref, y2_ref, route_ref, mod_ref, g_ref, b_ref, o_ref, *, alpha):
    m = mod_ref[0]
    rt = route_ref[...]
    tm = rt.shape[0]
    lane = lax.broadcasted_iota(jnp.int32, rt.shape, 1)
    g1 = jnp.sum(jnp.where(lane == 2, rt, 0.0), axis=-1, keepdims=True)
    g2 = jnp.sum(jnp.where(lane == 3, rt, 0.0), axis=-1, keepdims=True)
    f = _from_token_tiles(y1_ref, tm) * g1 + _from_token_tiles(y2_ref, tm) * g2
    o_ref[...] = _layer_norm_rows(alpha * x_ref[...] + m[5:6] * f, g_ref[...], b_ref[...])


def _moe_ln2(x1, pairs, route, mod, lw, geo, *, alpha):
    t_rows, d = x1.shape
    tm = geo["tm"]
    nt = t_rows // tm
    row = lambda i: (i, 0)
    return pl.pallas_call(
        functools.partial(_moe_ln2_kernel, alpha=alpha),
        grid=(nt,),
        in_specs=[pl.BlockSpec((tm, d), row), pl.BlockSpec((tm * TOKEN_TILE, LANES), row),
                  pl.BlockSpec((tm * TOKEN_TILE, LANES), lambda i: (nt + i, 0)), pl.BlockSpec((tm, LANES), row),
                  pl.BlockSpec((1, 6, d), geo["mod_map"]), _resident((1, d)), _resident((1, d))],
        out_specs=pl.BlockSpec((tm, d), row),
        out_shape=jax.ShapeDtypeStruct((t_rows, d), F32),
        compiler_params=_cparams(("parallel",)),
        name="moe_combine_ln2",
    )(x1, pairs, pairs, route, mod, lw["ln2_g"], lw["ln2_b"])


def _moe_layer(x1, h2_tiles, route, mod, lw, geo, *, alpha):
    t_rows = x1.shape[0]
    mb = geo["mb"]
    n_assign = 2 * t_rows
    n_blocks = -(-(n_assign + N_EXPERTS * (mb - 1)) // mb)
    expert_of = route[:, 0:2].astype(jnp.int32).reshape(n_assign)
    onehot = (expert_of[:, None] == jnp.arange(N_EXPERTS, dtype=jnp.int32)[None, :]).astype(jnp.int32)
    csum = jnp.cumsum(onehot, axis=0)
    rank = jnp.take_along_axis(csum, expert_of[:, None], axis=1)[:, 0] - 1
    counts = csum[-1]
    padded = (counts + mb - 1) // mb * mb
    padded_end = jnp.cumsum(padded)
    dest = (padded_end - padded)[expert_of] + rank
    n_active = (padded_end[-1] // mb).astype(jnp.int32).reshape(1)
    block_expert = jnp.minimum(
        jnp.searchsorted(padded_end, jnp.arange(n_blocks, dtype=jnp.int32) * mb, side="right"),
        N_EXPERTS - 1).astype(jnp.int32)
    token_of_row = jnp.zeros((n_blocks * mb,), jnp.int32).at[dest].set(
        jnp.arange(n_assign, dtype=jnp.int32) // 2)
    rows = _gather_rows(h2_tiles, token_of_row, mb)
    out = _ffn_grouped(rows, lw["moe_wg"], lw["moe_wu"], lw["moe_wd"], block_expert, n_active, geo)
    pair_idx = jnp.concatenate([dest[0::2], dest[1::2]]).astype(jnp.int32)
    pairs = _gather_rows(out, pair_idx, geo["tm"])
    return _moe_ln2(x1, pairs, route, mod, lw, geo, alpha=alpha)


def _rope_tables(n_lat, tm):
    pos = jnp.arange(n_lat, dtype=jnp.int32)
    row = (pos // GRID_W).astype(F32)
    col = (pos % GRID_W).astype(F32)

    def axial(rot_dim):
        axis_dim = rot_dim // 2
        inv = ROPE_BASE ** (-jnp.arange(0, axis_dim, 2, dtype=F32) / axis_dim)
        ang_r = row[:, None] * inv[None, :]
        ang_c = col[:, None] * inv[None, :]
        ang = jnp.concatenate([ang_r, ang_r, ang_c, ang_c], axis=-1)
        return jnp.cos(ang), jnp.sin(ang)

    def pad(x, width, fill=0.0):
        return jnp.pad(x, ((0, 0), (0, width - x.shape[1])), constant_values=fill)

    def signed(sin, half):
        first = (jnp.arange(sin.shape[1]) % (2 * half)) < half
        return jnp.where(first[None, :], -sin, sin)

    cd, sd = axial(DA_HEAD_DIM)
    cm, sm = axial(MLA_ROPE)
    sd_s, sm_s = signed(sd, DA_HEAD_DIM // 4), signed(sm, MLA_ROPE // 4)
    ones_nope = jnp.ones((n_lat, MLA_NOPE), F32)
    zeros_nope = jnp.zeros((n_lat, MLA_NOPE), F32)
    lat = jnp.concatenate([
        pad(jnp.concatenate([cd, cd], -1), LANES), pad(jnp.concatenate([sd_s, sd_s], -1), LANES),
        pad(jnp.concatenate([ones_nope, cm], -1), LANES), pad(jnp.concatenate([zeros_nope, sm_s], -1), LANES),
        pad(jnp.concatenate([cm, sm], -1), LANES)], axis=-1)
    ident_row = jnp.concatenate([
        pad(jnp.ones((1, 2 * DA_HEAD_DIM), F32), LANES), jnp.zeros((1, LANES), F32),
        pad(jnp.ones((1, MLA_NOPE + MLA_ROPE), F32), LANES), jnp.zeros((1, LANES), F32),
        pad(jnp.ones((1, MLA_ROPE), F32), LANES)], axis=-1)
    return jnp.concatenate([lat, jnp.broadcast_to(ident_row, (tm, 5 * LANES))], axis=0)


def _rot_cols(w, dim):
    lead = w.shape[:-1]
    g = w.reshape(lead + (-1, 2, dim // 2))
    return jnp.concatenate([-g[..., 1:2, :], g[..., 0:1, :]], axis=-2).reshape(w.shape)


def _to_slots(w, heads, width):
    lead = w.shape[:-1]
    g = w.reshape(lead + (heads, width))
    g = jnp.pad(g, [(0, 0)] * len(lead) + [(0, 0), (0, LANES - width)])
    return g.reshape(lead + (heads * LANES,))


def _layer_weights(l, p):
    d = p["w_in"].shape[1]
    w_in = p["w_in"][l]
    da = w_in[:, :3 * DA_Q]
    wq, wk, wv = da[:, :DA_Q], da[:, DA_Q:2 * DA_Q], da[:, 2 * DA_Q:]
    hy = w_in[:, 3 * DA_Q:3 * DA_Q + HY_IN]
    mla = w_in[:, 3 * DA_Q + HY_IN:]
    w_cq = mla[:, :MLA_Q_RANK]
    w_ckv = mla[:, MLA_Q_RANK:MLA_Q_RANK + MLA_KV_RANK]
    w_kr = mla[:, MLA_Q_RANK + MLA_KV_RANK:]
    kr_seg = jnp.pad(jnp.concatenate([w_kr, _rot_cols(w_kr, MLA_ROPE // 2)], -1),
                     ((0, 0), (0, LANES - 2 * MLA_ROPE)))
    w_in_ext = jnp.concatenate([
        _to_slots(wq, DA_HEADS, DA_V_DIM), _to_slots(wk, DA_HEADS, DA_V_DIM),
        _to_slots(wv, DA_HEADS, DA_V_DIM), hy, w_cq, w_ckv, kr_seg], axis=-1).astype(BF16)

    qh = MLA_NOPE + MLA_ROPE
    wuq = p["mla_w_uq"][l].reshape(MLA_Q_RANK, MLA_HEADS, qh)
    wukv = p["mla_w_ukv"][l].reshape(MLA_KV_RANK, MLA_HEADS, MLA_NOPE + MLA_V)
    wuk = _to_slots(wukv[..., :MLA_NOPE].reshape(MLA_KV_RANK, -1), MLA_HEADS, MLA_NOPE)
    wuv = _to_slots(wukv[..., MLA_NOPE:].reshape(MLA_KV_RANK, -1), MLA_HEADS, MLA_V)
    lane_src = jnp.arange(LANES)[:, None]
    lane_dst = jnp.arange(MLA_SLOTS)[None, :] % LANES
    e2 = ((lane_src < 2 * MLA_ROPE) & (lane_dst == MLA_NOPE + lane_src % MLA_ROPE)).astype(BF16)

    slot_lane = jnp.arange(DA_SLOTS) % LANES
    vb_da = (slot_lane == DA_V_DIM).astype(F32)[None, :]
    vb_mla = ((jnp.arange(MLA_SLOTS) % LANES) == MLA_V).astype(F32)[None, :]

    w_out = p["w_out"][l]
    w_out_ext = jnp.concatenate([
        _to_slots(w_out[:DA_HEADS * DA_V_DIM].T, DA_HEADS, DA_V_DIM).T,
        w_out[DA_HEADS * DA_V_DIM:DA_HEADS * DA_V_DIM + HY_CH],
        _to_slots(w_out[DA_HEADS * DA_V_DIM + HY_CH:].T, MLA_HEADS, MLA_V).T], axis=0).astype(BF16)

    def pad_lanes(v):
        return jnp.pad(v, (0, LANES - v.shape[0]))[None, :]

    lam = jnp.concatenate([pad_lanes(p[k][l]) for k in
                           ("da_lambda_q1", "da_lambda_k1", "da_lambda_q2", "da_lambda_k2")], axis=0)
    hw = {
        "conv_w": p["hy_conv_w"][l][:, None, :], "conv_b": p["hy_conv_b"][l][None, :],
        "fw1": jnp.pad(p["hy_fw1"][l], ((0, HY_HIDDEN - HY_EMB), (0, 0))), "fb1": p["hy_fb1"][l][None, :],
        "fw2": p["hy_fw2"][l], "fb2": p["hy_fb2"][l][None, :],
        "fw3": p["hy_fw3"][l], "fb3": p["hy_fb3"][l][None, :],
        "fw4": p["hy_fw4"][l], "freq": p["hy_freq"][l][None, :], "bias": p["hy_bias"][l][:, None, :],
    }
    return {
        "w_in": w_in_ext, "qg": p["mla_q_norm_g"][l][None, :], "kvg": p["mla_kv_norm_g"][l][None, :],
        "wuq": _to_slots(wuq.reshape(MLA_Q_RANK, -1), MLA_HEADS, qh).astype(BF16),
        "wukv": jnp.concatenate([wuk, wuv], -1).astype(BF16), "e2": e2,
        "vb_da": vb_da, "vb_mla": vb_mla, "w_out": w_out_ext, "lam": lam,
        "subln_g": pad_lanes(p["da_subln_g"][l]), "hy": hw,
        "ln1_g": p["ln1_g"][l][None, :], "ln1_b": p["ln1_b"][l][None, :],
        "ln2_g": p["ln2_g"][l][None, :], "ln2_b": p["ln2_b"][l][None, :],
    }


def _hyena_consts(n):
    t = jnp.linspace(0.0, 1.0, n, dtype=F32)[:, None]
    bands = (HY_EMB - 1) // 2
    w = 2.0 * math.pi * jnp.arange(n, dtype=F32)[:, None] / n
    f = jnp.linspace(1e-4, bands - 1, bands, dtype=F32)[None, :]
    z = jnp.concatenate([t, jnp.cos(f * w), -jnp.sin(f * w)], axis=-1)
    z = jnp.pad(z, ((0, 0), (0, HY_HIDDEN - HY_EMB)))
    max_decay = math.log(HY_DECAY_TARGET) / HY_DECAY_SHORT_PCT
    min_decay = math.log(HY_DECAY_TARGET) / HY_DECAY_LONG_PCT
    deltas = jnp.linspace(min_decay, max_decay, HY_CH, dtype=F32)
    win = jnp.exp(-t * jnp.abs(deltas)[None, :]) + HY_DECAY_SHIFT
    k = jnp.arange(n, dtype=jnp.int32)
    ang = ((k[:, None] * k[None, :]) % (2 * n)).astype(F32) * (math.pi / n)
    sgn = (1 - 2 * (k % 2)).astype(F32)[:, None]
    return {"z": z, "win": win, "sgn": sgn, "c": jnp.cos(ang).astype(BF16), "s": jnp.sin(ang).astype(BF16)}


def _pick_tile(limit, *sizes):
    t = limit
    while any(s % t for s in sizes):
        t //= 2
    return t


def _make_geo(b, nl, nc):
    t_lat = b * nl
    tm = _pick_tile(512, nl, b * nc)
    tm_ffn = _pick_tile(512, nl, b * nc)
    geo = {"b": b, "nl": nl, "nc": nc, "tm": tm, "tm_ffn": tm_ffn, "fsub": 256,
           "tq": _pick_tile(1024, nl), "mb": 512, "hps_da": 4, "hps_mla": 6}
    n_lat_t, per_b = t_lat // tm, nl // tm
    geo["mod_map"] = lambda i: (jnp.where(i < n_lat_t, i // per_b, b), 0, 0)
    geo["tab_map"] = lambda i: (jnp.where(i < n_lat_t, i % per_b, per_b), 0)
    n_lat_f, per_b_f = t_lat // tm_ffn, nl // tm_ffn
    geo["mod_map_ffn"] = lambda i: (jnp.where(i < n_lat_f, i // per_b_f, b), 0, 0)
    return geo


def kernel(x, c, ctx, c_ctx, ada_w, ada_b, w_in, da_lambda_q1, da_lambda_k1, da_lambda_q2, da_lambda_k2, da_subln_g, hy_conv_w, hy_conv_b, hy_fw1, hy_fb1, hy_fw2, hy_fb2, hy_fw3, hy_fb3, hy_fw4, hy_freq, hy_bias, mla_q_norm_g, mla_w_uq, mla_kv_norm_g, mla_w_ukv, w_out, ln1_g, ln1_b, ln2_g, ln2_b, ffn_w_gate, ffn_w_up, ffn_w_down, moe_router, moe_w_gate, moe_w_up, moe_w_down):
    p = dict(w_in=w_in, da_lambda_q1=da_lambda_q1, da_lambda_k1=da_lambda_k1, da_lambda_q2=da_lambda_q2,
             da_lambda_k2=da_lambda_k2, da_subln_g=da_subln_g, hy_conv_w=hy_conv_w, hy_conv_b=hy_conv_b,
             hy_fw1=hy_fw1, hy_fb1=hy_fb1, hy_fw2=hy_fw2, hy_fb2=hy_fb2, hy_fw3=hy_fw3, hy_fb3=hy_fb3,
             hy_fw4=hy_fw4, hy_freq=hy_freq, hy_bias=hy_bias, mla_q_norm_g=mla_q_norm_g, mla_w_uq=mla_w_uq,
             mla_kv_norm_g=mla_kv_norm_g, mla_w_ukv=mla_w_ukv, w_out=w_out, ln1_g=ln1_g, ln1_b=ln1_b,
             ln2_g=ln2_g, ln2_b=ln2_b)
    b, nl, d = x.shape
    nc = ctx.shape[1]
    depth = ada_w.shape[0]
    assert nl % nc == 0 and nl % GRID_W == 0 and nc % 8 == 0
    alpha = (2 * depth) ** 0.25
    t_lat, t_all = b * nl, b * (nl + nc)

    geo = _make_geo(b, nl, nc)
    tm = geo["tm"]

    bp = -(-(b + 1) // 8) * 8
    c_all = jnp.zeros((bp, d), F32).at[:b].set(c).at[b].set(c_ctx)
    mod_all = _ada_all(c_all, ada_w, ada_b).reshape(depth, bp, 6, d)

    tab = _rope_tables(nl, tm)
    consts_lat = _hyena_consts(nl)
    consts_ctx = _hyena_consts(nc)
    xs = jnp.concatenate([x.reshape(t_lat, d), ctx.reshape(b * nc, d)], axis=0)

    for l in range(depth):
        need_ctx = l < depth - 1
        lw = _layer_weights(l, p)
        mod = mod_all[l]
        lambda_init = 0.8 - 0.6 * math.exp(-0.3 * l)

        daq, dak, dav, hy_p, mlaq, mlak, mlav = _in_proj(xs, mod, lw, tab, geo)
        da_kw = dict(heads=DA_HEADS, hps=geo["hps_da"], n_maps=2, dv=DA_V_DIM, post_scale=1.0 - lambda_init,
                     lam=lw["lam"], gain=lw["subln_g"])
        mla_kw = dict(heads=MLA_HEADS, hps=geo["hps_mla"], n_maps=1, dv=MLA_V)
        mix_lat = (_attention(daq, dak, dav, geo, lat_queries=True, **da_kw),
                   _hy_conv(hy_p, lw["hy"], _hy_filter(lw["hy"], consts_lat), consts_lat, geo, ctx=False),
                   _attention(mlaq, mlak, mlav, geo, lat_queries=True, **mla_kw))
        mix_ctx = None
        if need_ctx:
            mix_ctx = (_attention(daq, dak, dav, geo, lat_queries=False, **da_kw),
                       _hy_conv(hy_p, lw["hy"], _hy_filter(lw["hy"], consts_ctx), consts_ctx, geo, ctx=True),
                       _attention(mlaq, mlak, mlav, geo, lat_queries=False, **mla_kw))

        idx = l // 2
        if l % 2 == 0:
            x1, h2 = _out_proj(mix_lat, mix_ctx, xs, mod, lw, geo, alpha=alpha)
            xs = _ffn_dense(h2, ffn_w_gate[idx:idx + 1].astype(BF16), ffn_w_up[idx:idx + 1].astype(BF16),
                            ffn_w_down[idx:idx + 1].astype(BF16), x1, mod, lw, geo, alpha=alpha)
        else:
            router = jnp.pad(moe_router[idx], ((0, 0), (0, LANES - N_EXPERTS))).astype(BF16)
            x1, h2, route = _out_proj(mix_lat, mix_ctx, xs, mod, lw, geo, alpha=alpha, router=router)
            lw["moe_wg"] = moe_w_gate[idx].astype(BF16)
            lw["moe_wu"] = moe_w_up[idx].astype(BF16)
            lw["moe_wd"] = moe_w_down[idx].astype(BF16)
            xs = _moe_layer(x1, h2, route, mod, lw, geo, alpha=alpha)
    return xs[:t_lat].reshape(b, nl, d)
```

```python
import functools
import math

import jax
import jax.numpy as jnp
from jax import lax
from jax.experimental import pallas as pl
from jax.experimental.pallas import tpu as pltpu

F32 = jnp.float32
BF16 = jnp.bfloat16

GRID_W = 64
DA_HEADS = 4
DA_HEAD_DIM = 48
DA_V_DIM = 2 * DA_HEAD_DIM
HY_CH = 256
HY_ORDER = 2
HY_EMB = 33
HY_HIDDEN = 64
HY_DECAY_TARGET = 1e-2
HY_DECAY_SHORT_PCT = 0.3
HY_DECAY_LONG_PCT = 1.5
HY_DECAY_SHIFT = 0.05
MLA_HEADS = 6
MLA_Q_RANK = 256
MLA_KV_RANK = 128
MLA_NOPE = 64
MLA_ROPE = 32
MLA_V = 64
N_EXPERTS = 8
ROPE_BASE = 10000.0
LN_EPS = 1e-5
RMS_EPS = 1e-6

LANES = 128
TOKEN_TILE = 8
VMEM_LIMIT = 56 * 1024 * 1024

DA_Q = DA_HEADS * 2 * DA_HEAD_DIM
DA_SLOTS = DA_HEADS * LANES
MLA_SLOTS = MLA_HEADS * LANES
HY_IN = (HY_ORDER + 1) * HY_CH
MIX_EXT = DA_SLOTS + HY_CH + MLA_SLOTS

_C_Q, _C_K, _C_V = 0, 512, 1024
_C_HY = 1536
_C_CQ = _C_HY + HY_IN
_C_CKV = _C_CQ + MLA_Q_RANK
_C_KROPE = _C_CKV + MLA_KV_RANK
IN_EXT = _C_KROPE + LANES


def _cparams(sem):
    return pltpu.CompilerParams(dimension_semantics=sem, vmem_limit_bytes=VMEM_LIMIT)


def _resident(shape):
    nd = len(shape)
    return pl.BlockSpec(shape, lambda *_: (0,) * nd, pipeline_mode=pl.Buffered(1))


def _dot(a, b):
    return jnp.dot(a, b, preferred_element_type=F32)


def _dot_nt(a, b):
    return lax.dot_general(a, b, (((1,), (1,)), ((), ())), preferred_element_type=F32)


def _layer_norm_rows(y, g, b):
    mu = jnp.mean(y, axis=-1, keepdims=True)
    yc = y - mu
    var = jnp.mean(yc * yc, axis=-1, keepdims=True)
    return yc * lax.rsqrt(var + LN_EPS) * g + b


def _sigmoid(x):
    return 1.0 / (1.0 + jnp.exp(-x))


def _ada_kernel(c_ref, w_ref, b_ref, o_ref):
    c = c_ref[...]
    act = (c * _sigmoid(c)).astype(BF16)
    o_ref[0] = _dot(act, w_ref[0].astype(BF16)) + b_ref[0]


def _ada_all(c_all, ada_w, ada_b):
    depth, d, n6 = ada_w.shape
    bp = c_all.shape[0]
    tn = 512
    return pl.pallas_call(
        _ada_kernel,
        grid=(depth, n6 // tn),
        in_specs=[
            pl.BlockSpec((bp, d), lambda l, j: (0, 0)),
            pl.BlockSpec((1, d, tn), lambda l, j: (l, 0, j)),
            pl.BlockSpec((1, 1, tn), lambda l, j: (l, 0, j)),
        ],
        out_specs=pl.BlockSpec((1, bp, tn), lambda l, j: (l, 0, j)),
        out_shape=jax.ShapeDtypeStruct((depth, bp, n6), F32),
        compiler_params=_cparams(("parallel", "parallel")),
        name="ada_mod",
    )(c_all, ada_w, ada_b.reshape(depth, 1, n6))


def _tile_lanes(x, reps):
    return jnp.concatenate([x] * reps, axis=-1)


def _rms_rows(x, g):
    return x * lax.rsqrt(jnp.mean(x * x, axis=-1, keepdims=True) + RMS_EPS) * g


def _rotate_half_slots(x, slots, half, first_half):
    parts = []
    for h in range(slots):
        xs = x[:, h * LANES:(h + 1) * LANES]
        parts.append(jnp.where(first_half, pltpu.roll(xs, LANES - half, 1), pltpu.roll(xs, half, 1)))
    return jnp.concatenate(parts, axis=-1)


def _in_proj_kernel(x_ref, mod_ref, w_ref, tab_ref, qg_ref, kvg_ref, wuq_ref, wukv_ref,
                    e2_ref, vbda_ref, vbmla_ref,
                    daq_ref, dak_ref, dav_ref, hy_ref, mlaq_ref, mlak_ref, mlav_ref,
                    *, da_scale, mla_scale):
    m = mod_ref[0]
    h = (x_ref[...] * (1.0 + m[1:2]) + m[0:1]).astype(BF16)

    def seg(a, b):
        return _dot(h, w_ref[:, a:b])

    lane = lax.broadcasted_iota(jnp.int32, (h.shape[0], LANES), 1)
    da_half = DA_HEAD_DIM // 4
    da_first = (lane % (2 * da_half)) < da_half
    cos_da = _tile_lanes(tab_ref[:, 0:128], DA_HEADS)
    sin_da = _tile_lanes(tab_ref[:, 128:256], DA_HEADS)
    q = seg(_C_Q, _C_K)
    daq_ref[...] = ((q * cos_da + _rotate_half_slots(q, DA_HEADS, da_half, da_first) * sin_da)
                    * da_scale).astype(BF16)
    k = seg(_C_K, _C_V)
    dak_ref[...] = (k * cos_da + _rotate_half_slots(k, DA_HEADS, da_half, da_first) * sin_da).astype(BF16)
    dav_ref[...] = (seg(_C_V, _C_HY) + vbda_ref[...]).astype(BF16)
    hy_ref[...] = seg(_C_HY, _C_CQ)

    cqn = _rms_rows(seg(_C_CQ, _C_CKV), qg_ref[...]).astype(BF16)
    mla_half = MLA_ROPE // 4
    mla_first = ((lane - MLA_NOPE) % (2 * mla_half)) < mla_half
    cos_mq = _tile_lanes(tab_ref[:, 256:384], MLA_HEADS)
    sin_mq = _tile_lanes(tab_ref[:, 384:512], MLA_HEADS)
    mq = _dot(cqn, wuq_ref[...])
    mlaq_ref[...] = ((mq * cos_mq + _rotate_half_slots(mq, MLA_HEADS, mla_half, mla_first) * sin_mq)
                     * mla_scale).astype(BF16)

    ckvn = _rms_rows(seg(_C_CKV, _C_KROPE), kvg_ref[...]).astype(BF16)
    kv = _dot(ckvn, wukv_ref[...])
    krw = seg(_C_KROPE, IN_EXT) * tab_ref[:, 512:640]
    kr_hi = krw.astype(BF16)
    kr_lo = (krw - kr_hi.astype(F32)).astype(BF16)
    placed = _dot(kr_hi, e2_ref[...]) + _dot(kr_lo, e2_ref[...])
    mlak_ref[...] = (kv[:, :MLA_SLOTS] + placed).astype(BF16)
    mlav_ref[...] = (kv[:, MLA_SLOTS:] + vbmla_ref[...]).astype(BF16)


def _in_proj(xs, mod, lw, tab, geo):
    t_rows, d = xs.shape
    tm = geo["tm"]
    row = lambda i: (i, 0)
    outs = [(DA_SLOTS, BF16), (DA_SLOTS, BF16), (DA_SLOTS, BF16), (HY_IN, F32),
            (MLA_SLOTS, BF16), (MLA_SLOTS, BF16), (MLA_SLOTS, BF16)]
    kern = functools.partial(_in_proj_kernel, da_scale=DA_HEAD_DIM ** -0.5,
                             mla_scale=(MLA_NOPE + MLA_ROPE) ** -0.5)
    return pl.pallas_call(
        kern,
        grid=(t_rows // tm,),
        in_specs=[
            pl.BlockSpec((tm, d), row),
            pl.BlockSpec((1, 6, d), geo["mod_map"]),
            _resident((d, IN_EXT)),
            pl.BlockSpec((tm, 5 * LANES), geo["tab_map"]),
            _resident((1, MLA_Q_RANK)),
            _resident((1, MLA_KV_RANK)),
            _resident((MLA_Q_RANK, MLA_SLOTS)),
            _resident((MLA_KV_RANK, 2 * MLA_SLOTS)),
            _resident((LANES, MLA_SLOTS)),
            _resident((1, DA_SLOTS)),
            _resident((1, MLA_SLOTS)),
        ],
        out_specs=[pl.BlockSpec((tm, w), row) for w, _ in outs],
        out_shape=[jax.ShapeDtypeStruct((t_rows, w), dt) for w, dt in outs],
        compiler_params=_cparams(("parallel",)),
        name="in_proj",
    )(xs, mod, lw["w_in"], tab, lw["qg"], lw["kvg"], lw["wuq"], lw["wukv"],
      lw["e2"], lw["vb_da"], lw["vb_mla"])


def _attn_kernel(*refs, n_maps, dv, post_scale, lat_queries, hps):
    if n_maps == 2:
        lam_ref, g_ref, q_ref, kl_ref, kc_ref, vl_ref, vc_ref, o_ref = refs
    else:
        q_ref, kl_ref, kc_ref, vl_ref, vc_ref, o_ref = refs
    tq = q_ref.shape[0]
    lane = lax.broadcasted_iota(jnp.int32, (tq, LANES), 1)

    def attend(qm, hs, include_lat):
        sc = _dot_nt(qm, kc_ref[:, hs])
        mx = jnp.max(sc, axis=-1, keepdims=True)
        if include_lat:
            sl = _dot_nt(qm, kl_ref[:, hs])
            mx = jnp.maximum(mx, jnp.max(sl, axis=-1, keepdims=True))
            ol = _dot(jnp.exp(sl - mx).astype(BF16), vl_ref[:, hs])
        ol_c = _dot(jnp.exp(sc - mx).astype(BF16), vc_ref[:, hs])
        ol = ol + ol_c if include_lat else ol_c
        den = jnp.sum(jnp.where(lane == dv, ol, 0.0), axis=-1, keepdims=True)
        return ol / den

    def head(h, include_lat):
        hs = slice(h * LANES, (h + 1) * LANES)
        q = q_ref[:, hs]
        if n_maps == 2:
            q1 = jnp.where(lane < DA_HEAD_DIM, q, jnp.zeros_like(q))
            q2 = jnp.where(lane < DA_HEAD_DIM, jnp.zeros_like(q), q)
            lp = lam_ref[...]
            lam = (jnp.exp(jnp.sum(lp[0:1] * lp[1:2], axis=-1, keepdims=True))
                   - jnp.exp(jnp.sum(lp[2:3] * lp[3:4], axis=-1, keepdims=True))
                   + (1.0 - post_scale))
            o = attend(q1, hs, include_lat) - lam * attend(q2, hs, include_lat)
            o = jnp.where(lane < dv, o, 0.0)
            ms = jnp.sum(o * o, axis=-1, keepdims=True) * (1.0 / dv)
            return o * lax.rsqrt(ms + RMS_EPS) * g_ref[...] * post_scale
        return jnp.where(lane < dv, attend(q, hs, include_lat), 0.0)

    o = jnp.concatenate([head(h, lat_queries) for h in range(hps)], axis=-1)
    o_ref[...] = o.astype(o_ref.dtype)


def _attention(q, k, v, geo, *, heads, hps, n_maps, dv, lat_queries, post_scale=1.0, lam=None, gain=None):
    b, nl, nc = geo["b"], geo["nl"], geo["nc"]
    tq = geo["tq"] if lat_queries else nc
    n_q_tiles = (nl if lat_queries else nc) // tq
    q_blk0 = 0 if lat_queries else b * nl // nc
    ctx_kv0 = b * nl // nc
    width = hps * LANES
    kv_lat = pl.BlockSpec((nl if lat_queries else 8, width), lambda bi, hi, j: (bi if lat_queries else 0, hi))
    kv_ctx = pl.BlockSpec((nc, width), lambda bi, hi, j: (ctx_kv0 + bi, hi))
    in_specs = [pl.BlockSpec((tq, width), lambda bi, hi, j: (q_blk0 + bi * n_q_tiles + j, hi)),
                kv_lat, kv_ctx, kv_lat, kv_ctx]
    args = [q, k, k, v, v]
    if n_maps == 2:
        in_specs = [pl.BlockSpec((4, LANES), lambda bi, hi, j: (0, 0)),
                    pl.BlockSpec((1, LANES), lambda bi, hi, j: (0, 0))] + in_specs
        args = [lam, gain] + args
    kern = functools.partial(_attn_kernel, n_maps=n_maps, dv=dv, post_scale=post_scale,
                             lat_queries=lat_queries, hps=hps)
    return pl.pallas_call(
        kern,
        grid=(b, heads // hps, n_q_tiles),
        in_specs=in_specs,
        out_specs=pl.BlockSpec((tq, width), lambda bi, hi, j: (bi * n_q_tiles + j, hi)),
        out_shape=jax.ShapeDtypeStruct((b * n_q_tiles * tq, q.shape[1]), BF16),
        compiler_params=_cparams(("parallel", "parallel", "arbitrary")),
        name=("diff_attn" if n_maps == 2 else "mla_attn") + ("_lat" if lat_queries else "_ctx"),
    )(*args)


def _hy_filter_kernel(z_ref, w1_ref, b1_ref, w2_ref, b2_ref, w3_ref, b3_ref, w4_ref, fr_ref,
                      win_ref, sgn_ref, a_ref, d_ref, hn_ref, *, rows):
    hp = lax.Precision.HIGHEST
    fr = fr_ref[...]

    def lin(x, w_ref, b_ref):
        return jnp.dot(x, w_ref[...], precision=hp, preferred_element_type=F32) + b_ref[...]

    hdn = jnp.sin(fr * lin(z_ref[...], w1_ref, b1_ref))
    hdn = jnp.sin(fr * lin(hdn, w2_ref, b2_ref))
    hdn = jnp.sin(fr * lin(hdn, w3_ref, b3_ref))
    filt = jnp.dot(hdn, w4_ref[...], precision=hp, preferred_element_type=F32)
    win = _tile_lanes(win_ref[...], HY_ORDER)
    half = HY_ORDER * HY_CH
    fwd = filt[:, :half] * win
    bwd = filt[:, half:] * win
    i = pl.program_id(0)
    row = lax.broadcasted_iota(jnp.int32, bwd.shape, 0) + i * rows
    bwd = jnp.where(row == 0, 0.0, bwd)
    a = fwd + bwd
    a_ref[...] = a.astype(BF16)
    d_ref[...] = (bwd - fwd).astype(BF16)

    @pl.when(i == 0)
    def _():
        hn_ref[...] = jnp.zeros_like(hn_ref)

    hn_ref[...] += jnp.sum(a * sgn_ref[...], axis=0, keepdims=True)


def _hy_spectrum_kernel(c_ref, s_ref, a_ref, d_ref, hr_ref, hi_ref):
    hr_ref[...] = _dot(c_ref[...], a_ref[...])
    hi_ref[...] = _dot(s_ref[...], d_ref[...])


def _hy_filter(hw, consts):
    n = consts["c"].shape[0]
    half = HY_ORDER * HY_CH
    tr = _pick_tile(512, n)
    row = lambda i: (i, 0)
    weights = (hw["fw1"], hw["fb1"], hw["fw2"], hw["fb2"], hw["fw3"], hw["fb3"], hw["fw4"], hw["freq"])
    a, d, hn = pl.pallas_call(
        functools.partial(_hy_filter_kernel, rows=tr),
        grid=(n // tr,),
        in_specs=([pl.BlockSpec((tr, HY_HIDDEN), row)] + [_resident(w.shape) for w in weights]
                  + [pl.BlockSpec((tr, HY_CH), row), pl.BlockSpec((tr, 1), row)]),
        out_specs=[pl.BlockSpec((tr, half), row), pl.BlockSpec((tr, half), row),
                   pl.BlockSpec((1, half), lambda i: (0, 0))],
        out_shape=[jax.ShapeDtypeStruct((n, half), BF16), jax.ShapeDtypeStruct((n, half), BF16),
                   jax.ShapeDtypeStruct((1, half), F32)],
        compiler_params=_cparams(("arbitrary",)),
        name="hyena_filter",
    )(consts["z"], *weights, consts["win"], consts["sgn"])
    col = lambda j: (0, j)
    hr, hi = pl.pallas_call(
        _hy_spectrum_kernel,
        grid=(half // HY_CH,),
        in_specs=[_resident((n, n)), _resident((n, n)), pl.BlockSpec((n, HY_CH), col),
                  pl.BlockSpec((n, HY_CH), col)],
        out_specs=[pl.BlockSpec((n, HY_CH), col), pl.BlockSpec((n, HY_CH), col)],
        out_shape=[jax.ShapeDtypeStruct((n, half), F32), jax.ShapeDtypeStruct((n, half), F32)],
        compiler_params=_cparams(("parallel",)),
        name="hyena_spectrum",
    )(consts["c"], consts["s"], a, d)
    return hr, hi, hn


def _hy_conv_kernel(*refs, n, kc, rc):
    (p_ref, cw_ref, cb_ref, c_ref, s_ref, hr_ref, hi_ref, hn_ref, sgn_ref, bias_ref, o_ref,
     z_s, zb_s, y_s) = refs
    ri = lax.broadcasted_iota(jnp.int32, (rc, HY_CH), 0)
    zero_row = jnp.zeros((1, HY_CH), F32)

    def short_conv(g, r0):
        cs = slice(g * HY_CH, (g + 1) * HY_CH)
        p = p_ref[r0:r0 + rc, cs]
        up = p_ref[r0 - 1:r0, cs] if r0 > 0 else zero_row
        dn = p_ref[r0 + rc:r0 + rc + 1, cs] if r0 + rc < n else zero_row
        prev = jnp.where(ri == 0, up, pltpu.roll(p, 1, 0))
        nxt = jnp.where(ri == rc - 1, dn, pltpu.roll(p, rc - 1, 0))
        return prev * cw_ref[0, :, cs] + p * cw_ref[1, :, cs] + nxt * cw_ref[2, :, cs] + cb_ref[:, cs]

    for r0 in range(0, n, rc):
        z_s[r0:r0 + rc, :] = short_conv(HY_ORDER, r0)
    for i in range(HY_ORDER):
        cs = slice(i * HY_CH, (i + 1) * HY_CH)
        zb_s[...] = z_s[...].astype(BF16)
        xn = jnp.sum(z_s[...] * sgn_ref[...], axis=0, keepdims=True)
        y_s[...] = sgn_ref[...] * (xn * hn_ref[:, cs] * (0.5 / n))
        for k0 in range(0, n, kc):
            ks = slice(k0, k0 + kc)
            zb = zb_s[...]
            a = _dot(c_ref[ks, :], zb)
            bm = _dot(s_ref[ks, :], zb)
            hr = hr_ref[ks, cs]
            hi = hi_ref[ks, cs]
            yr = a * hr + bm * hi
            if k0 == 0:
                yr = jnp.where(lax.broadcasted_iota(jnp.int32, yr.shape, 0) == 0, 0.5 * yr, yr)
            yi = a * hi - bm * hr
            y_s[...] += (_dot(c_ref[:, ks], yr.astype(BF16)) - _dot(s_ref[:, ks], yi.astype(BF16))) * (1.0 / n)
        dst = o_ref if i == HY_ORDER - 1 else z_s
        for r0 in range(0, n, rc):
            rs = slice(r0, r0 + rc)
            dst[rs, :] = short_conv(i, r0) * (y_s[rs, :] + z_s[rs, :] * bias_ref[i])


def _hy_conv(hy_p, hw, spec, consts, geo, *, ctx):
    b, nl, nc = geo["b"], geo["nl"], geo["nc"]
    n = nc if ctx else nl
    blk0 = b * nl // nc if ctx else 0
    hr, hi, hn = spec
    args = [hy_p, hw["conv_w"], hw["conv_b"], consts["c"], consts["s"], hr, hi, hn, consts["sgn"],
            hw["bias"]]
    in_specs = ([pl.BlockSpec((n, HY_IN), lambda bi: (blk0 + bi, 0))]
                + [_resident(a.shape) for a in args[1:]])
    kern = functools.partial(_hy_conv_kernel, n=n, kc=_pick_tile(512, n), rc=_pick_tile(256, n))
    return pl.pallas_call(
        kern,
        grid=(b,),
        in_specs=in_specs,
        out_specs=pl.BlockSpec((n, HY_CH), lambda bi: (bi, 0)),
        out_shape=jax.ShapeDtypeStruct((b * n, HY_CH), F32),
        scratch_shapes=[pltpu.VMEM((n, HY_CH), F32), pltpu.VMEM((n, HY_CH), BF16), pltpu.VMEM((n, HY_CH), F32)],
        compiler_params=_cparams(("parallel",)),
        name="hyena_conv_ctx" if ctx else "hyena_conv_lat",
    )(*args)


def _top2_route(logits):
    lane = lax.broadcasted_iota(jnp.int32, logits.shape, 1)
    neg = jnp.float32(-jnp.inf)
    lg = jnp.where(lane < N_EXPERTS, logits, neg)
    m1 = jnp.max(lg, axis=-1, keepdims=True)
    i1 = jnp.min(jnp.where(lg == m1, lane, LANES), axis=-1, keepdims=True)
    lg2 = jnp.where(lane == i1, neg, lg)
    m2 = jnp.max(lg2, axis=-1, keepdims=True)
    i2 = jnp.min(jnp.where(lg2 == m2, lane, LANES), axis=-1, keepdims=True)
    e = jnp.exp(m2 - m1)
    g1 = 1.0 / (1.0 + e)
    g2 = e * g1
    out = jnp.where(lane == 0, i1.astype(F32), 0.0)
    out = jnp.where(lane == 1, i2.astype(F32), out)
    out = jnp.where(lane == 2, g1, out)
    return jnp.where(lane == 3, g2, out)


def _out_proj_kernel(*refs, alpha, moe, n_lat_tiles, has_ctx):
    n_mix = 6 if has_ctx else 3
    mix_refs, refs = refs[:n_mix], refs[n_mix:]
    if moe:
        w_ref, x_ref, mod_ref, g_ref, b_ref, rw_ref, x1_ref, h2_ref, route_ref = refs[:9]
    else:
        w_ref, x_ref, mod_ref, g_ref, b_ref, x1_ref, h2_ref = refs[:7]

    def mix(da_ref, hy_ref, mla_ref):
        return (_dot(da_ref[...], w_ref[0:DA_SLOTS])
                + _dot(hy_ref[...].astype(BF16), w_ref[DA_SLOTS:DA_SLOTS + HY_CH])
                + _dot(mla_ref[...], w_ref[DA_SLOTS + HY_CH:MIX_EXT]))

    if has_ctx:
        o_ref = refs[-1]
        is_lat = pl.program_id(0) < n_lat_tiles

        @pl.when(is_lat)
        def _():
            o_ref[...] = mix(*mix_refs[:3])

        @pl.when(jnp.logical_not(is_lat))
        def _():
            o_ref[...] = mix(*mix_refs[3:])

        o = o_ref[...]
    else:
        o = mix(*mix_refs)
    m = mod_ref[0]
    x1 = _layer_norm_rows(alpha * x_ref[...] + m[2:3] * o, g_ref[...], b_ref[...])
    x1_ref[...] = x1
    h2 = x1 * (1.0 + m[4:5]) + m[3:4]
    if moe:
        _to_token_tiles(h2_ref, h2)
        route_ref[...] = _top2_route(_dot(h2.astype(BF16), rw_ref[...]))
    else:
        h2_ref[...] = h2.astype(BF16)


def _out_proj(mix_lat, mix_ctx, xs, mod, lw, geo, *, alpha, router=None):
    d = xs.shape[1]
    tm = geo["tm"]
    row = lambda i: (i, 0)
    moe = router is not None
    has_ctx = mix_ctx is not None
    n_lat_tiles = mix_lat[0].shape[0] // tm
    t_rows = mix_lat[0].shape[0] + (mix_ctx[0].shape[0] if has_ctx else 0)
    widths = (DA_SLOTS, HY_CH, MLA_SLOTS)
    lat_row = lambda i: (jnp.minimum(i, n_lat_tiles - 1), 0)
    ctx_row = lambda i: (jnp.maximum(i - n_lat_tiles, 0), 0)
    in_specs = [pl.BlockSpec((tm, w), lat_row) for w in widths]
    args = list(mix_lat)
    if has_ctx:
        in_specs += [pl.BlockSpec((tm, w), ctx_row) for w in widths]
        args += list(mix_ctx)
    in_specs += [_resident((MIX_EXT, d)), pl.BlockSpec((tm, d), row),
                 pl.BlockSpec((1, 6, d), geo["mod_map"]), _resident((1, d)), _resident((1, d))]
    args += [lw["w_out"], xs, mod, lw["ln1_g"], lw["ln1_b"]]
    if moe:
        h2_spec = pl.BlockSpec((tm * TOKEN_TILE, LANES), row)
        h2_shape = jax.ShapeDtypeStruct((t_rows * TOKEN_TILE, LANES), F32)
    else:
        h2_spec, h2_shape = pl.BlockSpec((tm, d), row), jax.ShapeDtypeStruct((t_rows, d), BF16)
    out_specs = [pl.BlockSpec((tm, d), row), h2_spec]
    out_shape = [jax.ShapeDtypeStruct((t_rows, d), F32), h2_shape]
    if moe:
        in_specs.append(_resident((d, LANES)))
        args.append(router)
        out_specs.append(pl.BlockSpec((tm, LANES), row))
        out_shape.append(jax.ShapeDtypeStruct((t_rows, LANES), F32))
    return pl.pallas_call(
        functools.partial(_out_proj_kernel, alpha=alpha, moe=moe, n_lat_tiles=n_lat_tiles, has_ctx=has_ctx),
        grid=(t_rows // tm,),
        in_specs=in_specs,
        out_specs=out_specs,
        out_shape=out_shape,
        scratch_shapes=[pltpu.VMEM((tm, d), F32)] if has_ctx else [],
        compiler_params=_cparams(("parallel",)),
        name="out_proj_ln1",
    )(*args)


def _from_token_tiles(ref, rows):
    return jnp.concatenate([ref[pl.ds(c, rows, stride=TOKEN_TILE), :] for c in range(TOKEN_TILE)], axis=-1)


def _to_token_tiles(ref, val):
    rows = val.shape[0]
    for c in range(TOKEN_TILE):
        ref[pl.ds(c, rows, stride=TOKEN_TILE), :] = val[:, c * LANES:(c + 1) * LANES]


def _swiglu_kernel(*refs, alpha, grouped, fsub):
    if grouped:
        be_ref, na_ref, h_ref, wg_ref, wu_ref, wd_ref, o_ref, act_ref = refs
    else:
        h_ref, wg_ref, wu_ref, wd_ref, x_ref, mod_ref, g_ref, b_ref, o_ref, act_ref = refs

    def body():
        if grouped:
            h = _from_token_tiles(h_ref, act_ref.shape[0]).astype(BF16)
        else:
            h = h_ref[...]
        for c0 in range(0, wg_ref.shape[2], fsub):
            cs = slice(c0, c0 + fsub)
            g = _dot(h, wg_ref[0, :, cs])
            u = _dot(h, wu_ref[0, :, cs])
            act_ref[:, cs] = (g * _sigmoid(g) * u).astype(BF16)
        y = _dot(act_ref[...], wd_ref[0])
        if grouped:
            _to_token_tiles(o_ref, y)
        else:
            m = mod_ref[0]
            o_ref[...] = _layer_norm_rows(alpha * x_ref[...] + m[5:6] * y, g_ref[...], b_ref[...])

    if grouped:
        active = pl.program_id(0) < na_ref[0]
        pl.when(active)(body)

        @pl.when(jnp.logical_not(active))
        def _():
            o_ref[...] = jnp.zeros_like(o_ref)
    else:
        body()


def _ffn_dense(h2, wg, wu, wd, x1, mod, lw, geo, *, alpha):
    t_rows, d = h2.shape
    f_dim = wg.shape[-1]
    tm = geo["tm_ffn"]
    row = lambda i: (i, 0)
    return pl.pallas_call(
        functools.partial(_swiglu_kernel, alpha=alpha, grouped=False, fsub=geo["fsub"]),
        grid=(t_rows // tm,),
        in_specs=[pl.BlockSpec((tm, d), row), _resident((1, d, f_dim)), _resident((1, d, f_dim)),
                  _resident((1, f_dim, d)), pl.BlockSpec((tm, d), row),
                  pl.BlockSpec((1, 6, d), geo["mod_map_ffn"]), _resident((1, d)), _resident((1, d))],
        out_specs=pl.BlockSpec((tm, d), row),
        out_shape=jax.ShapeDtypeStruct((t_rows, d), F32),
        scratch_shapes=[pltpu.VMEM((tm, f_dim), BF16)],
        compiler_params=_cparams(("parallel",)),
        name="ffn_swiglu_ln2",
    )(h2, wg, wu, wd, x1, mod, lw["ln2_g"], lw["ln2_b"])


def _ffn_grouped(rows, wg, wu, wd, block_expert, n_active, geo, *, first_expert):
    d, f_dim = wg.shape[1], wg.shape[2]
    mb = geo["mb"]
    r_rows = rows.shape[0] // TOKEN_TILE
    blk = lambda i, be, na: (jnp.minimum(i, na[0] - 1), 0)
    wmap = lambda i, be, na: (first_expert + be[jnp.minimum(i, na[0] - 1)], 0, 0)
    grid_spec = pltpu.PrefetchScalarGridSpec(
        num_scalar_prefetch=2,
        grid=(r_rows // mb,),
        in_specs=[pl.BlockSpec((mb * TOKEN_TILE, LANES), blk),
                  pl.BlockSpec((1, d, f_dim), wmap), pl.BlockSpec((1, d, f_dim), wmap),
                  pl.BlockSpec((1, f_dim, d), wmap)],
        out_specs=pl.BlockSpec((mb * TOKEN_TILE, LANES), lambda i, be, na: (i, 0)),
        scratch_shapes=[pltpu.VMEM((mb, f_dim), BF16)],
    )
    return pl.pallas_call(
        functools.partial(_swiglu_kernel, alpha=0.0, grouped=True, fsub=geo["fsub"]),
        grid_spec=grid_spec,
        out_shape=jax.ShapeDtypeStruct(rows.shape, F32),
        compiler_params=_cparams(("arbitrary",)),
        name="moe_experts",
    )(block_expert, n_active, rows, wg, wu, wd)


def _gather_kernel(idx_ref, src_ref, o_ref, sem, *, rows):
    group = 8

    def start(g, carry):
        for j in range(group):
            r = g * group + j
            src = pl.multiple_of(idx_ref[r] * TOKEN_TILE, TOKEN_TILE)
            dst = pl.multiple_of(r * TOKEN_TILE, TOKEN_TILE)
            pltpu.make_async_copy(src_ref.at[pl.ds(src, TOKEN_TILE)], o_ref.at[pl.ds(dst, TOKEN_TILE)],
                                  sem).start(priority=j % 2)
        return carry

    lax.fori_loop(0, rows // group, start, 0)
    pltpu.make_async_copy(o_ref, o_ref, sem).wait()


def _gather_rows(src, idx, rows_per_step):
    m = idx.shape[0]
    return pl.pallas_call(
        functools.partial(_gather_kernel, rows=rows_per_step),
        grid=(m // rows_per_step,),
        in_specs=[pl.BlockSpec((rows_per_step,), lambda i: (i,), memory_space=pltpu.SMEM),
                  pl.BlockSpec(memory_space=pl.ANY)],
        out_specs=pl.BlockSpec((rows_per_step * TOKEN_TILE, LANES), lambda i: (i, 0)),
        out_shape=jax.ShapeDtypeStruct((m * TOKEN_TILE, LANES), src.dtype),
        scratch_shapes=[pltpu.SemaphoreType.DMA(())],
        compiler_params=_cparams(("arbitrary",)),
        name="gather_rows",
    )(idx, src)


def _dispatch_kernel(dest_ref, init_ref, src_ref, o_ref, sem, *, tokens):
    group = 8

    def start(g, carry):
        for j in range(group):
            a = g * group + j
            src = pl.multiple_of((g * (group // 2) + j // 2) * TOKEN_TILE, TOKEN_TILE)
            dst = pl.multiple_of(dest_ref[a] * TOKEN_TILE, TOKEN_TILE)
            pltpu.make_async_copy(src_ref.at[pl.ds(src, TOKEN_TILE)], o_ref.at[pl.ds(dst, TOKEN_TILE)],
                                  sem).start(priority=j % 2)
        return carry

    lax.fori_loop(0, 2 * tokens // group, start, 0)
    pltpu.make_async_copy(src_ref, src_ref, sem).wait()
    pltpu.make_async_copy(src_ref, src_ref, sem).wait()


def _dispatch_rows(src, dest, init, tokens_per_step):
    n_tok = src.shape[0] // TOKEN_TILE
    return pl.pallas_call(
        functools.partial(_dispatch_kernel, tokens=tokens_per_step),
        grid=(n_tok // tokens_per_step,),
        in_specs=[pl.BlockSpec((2 * tokens_per_step,), lambda i: (i,), memory_space=pltpu.SMEM),
                  pl.BlockSpec(memory_space=pl.ANY),
                  pl.BlockSpec((tokens_per_step * TOKEN_TILE, LANES), lambda i: (i, 0))],
        out_specs=pl.BlockSpec(memory_space=pl.ANY),
        out_shape=jax.ShapeDtypeStruct(init.shape, init.dtype),
        scratch_shapes=[pltpu.SemaphoreType.DMA(())],
        input_output_aliases={1: 0},
        compiler_params=_cparams(("arbitrary",)),
        name="dispatch_rows",
    )(dest, init, src)


def _moe_ln2_kernel(x_ref, y1_ref, y2_ref, route_ref, mod_ref, g_ref, b_ref, o_ref, *, alpha):
    m = mod_ref[0]
    rt = route_ref[...]
    tm = rt.shape[0]
    lane = lax.broadcasted_iota(jnp.int32, rt.shape, 1)
    g1 = jnp.sum(jnp.where(lane == 2, rt, 0.0), axis=-1, keepdims=True)
    g2 = jnp.sum(jnp.where(lane == 3, rt, 0.0), axis=-1, keepdims=True)
    f = _from_token_tiles(y1_ref, tm) * g1 + _from_token_tiles(y2_ref, tm) * g2
    o_ref[...] = _layer_norm_rows(alpha * x_ref[...] + m[5:6] * f, g_ref[...], b_ref[...])


def _moe_ln2(x1, pairs, route, mod, lw, geo, *, alpha):
    t_rows, d = x1.shape
    tm = geo["tm"]
    nt = t_rows // tm
    row = lambda i: (i, 0)
    return pl.pallas_call(
        functools.partial(_moe_ln2_kernel, alpha=alpha),
        grid=(nt,),
        in_specs=[pl.BlockSpec((tm, d), row), pl.BlockSpec((tm * TOKEN_TILE, LANES), row),
                  pl.BlockSpec((tm * TOKEN_TILE, LANES), lambda i: (nt + i, 0)), pl.BlockSpec((tm, LANES), row),
                  pl.BlockSpec((1, 6, d), geo["mod_map"]), _resident((1, d)), _resident((1, d))],
        out_specs=pl.BlockSpec((tm, d), row),
        out_shape=jax.ShapeDtypeStruct((t_rows, d), F32),
        compiler_params=_cparams(("parallel",)),
        name="moe_combine_ln2",
    )(x1, pairs, pairs, route, mod, lw["ln2_g"], lw["ln2_b"])


def _moe_layer(x1, h2_tiles, route, mod, lw, geo, *, alpha):
    t_rows = x1.shape[0]
    mb = geo["mb"]
    n_assign = 2 * t_rows
    n_blocks = -(-(n_assign + N_EXPERTS * (mb - 1)) // mb)
    expert_of = route[:, 0:2].astype(jnp.int32).reshape(n_assign)
    onehot = (expert_of[:, None] == jnp.arange(N_EXPERTS, dtype=jnp.int32)[None, :]).astype(jnp.int32)
    csum = jnp.cumsum(onehot, axis=0)
    rank = jnp.take_along_axis(csum, expert_of[:, None], axis=1)[:, 0] - 1
    counts = csum[-1]
    padded = (counts + mb - 1) // mb * mb
    padded_end = jnp.cumsum(padded)
    dest = (padded_end - padded)[expert_of] + rank
    n_active = (padded_end[-1] // mb).astype(jnp.int32).reshape(1)
    block_expert = jnp.minimum(
        jnp.searchsorted(padded_end, jnp.arange(n_blocks, dtype=jnp.int32) * mb, side="right"),
        N_EXPERTS - 1).astype(jnp.int32)
    rows = _dispatch_rows(h2_tiles, dest.astype(jnp.int32), jnp.zeros((n_blocks * mb * TOKEN_TILE, LANES), F32),
                          geo["tm"])
    out = _ffn_grouped(rows, lw["moe_wg"], lw["moe_wu"], lw["moe_wd"], block_expert, n_active, geo,
                       first_expert=lw["moe_first_expert"])
    pair_idx = jnp.concatenate([dest[0::2], dest[1::2]]).astype(jnp.int32)
    pairs = _gather_rows(out, pair_idx, geo["tm"])
    return _moe_ln2(x1, pairs, route, mod, lw, geo, alpha=alpha)


def _rope_tables(n_lat, tm):
    pos = jnp.arange(n_lat, dtype=jnp.int32)
    row = (pos // GRID_W).astype(F32)
    col = (pos % GRID_W).astype(F32)

    def axial(rot_dim):
        axis_dim = rot_dim // 2
        inv = ROPE_BASE ** (-jnp.arange(0, axis_dim, 2, dtype=F32) / axis_dim)
        ang_r = row[:, None] * inv[None, :]
        ang_c = col[:, None] * inv[None, :]
        ang = jnp.concatenate([ang_r, ang_r, ang_c, ang_c], axis=-1)
        return jnp.cos(ang), jnp.sin(ang)

    def pad(x, width, fill=0.0):
        return jnp.pad(x, ((0, 0), (0, width - x.shape[1])), constant_values=fill)

    def signed(sin, half):
        first = (jnp.arange(sin.shape[1]) % (2 * half)) < half
        return jnp.where(first[None, :], -sin, sin)

    cd, sd = axial(DA_HEAD_DIM)
    cm, sm = axial(MLA_ROPE)
    sd_s, sm_s = signed(sd, DA_HEAD_DIM // 4), signed(sm, MLA_ROPE // 4)
    ones_nope = jnp.ones((n_lat, MLA_NOPE), F32)
    zeros_nope = jnp.zeros((n_lat, MLA_NOPE), F32)
    lat = jnp.concatenate([
        pad(jnp.concatenate([cd, cd], -1), LANES), pad(jnp.concatenate([sd_s, sd_s], -1), LANES),
        pad(jnp.concatenate([ones_nope, cm], -1), LANES), pad(jnp.concatenate([zeros_nope, sm_s], -1), LANES),
        pad(jnp.concatenate([cm, sm], -1), LANES)], axis=-1)
    ident_row = jnp.concatenate([
        pad(jnp.ones((1, 2 * DA_HEAD_DIM), F32), LANES), jnp.zeros((1, LANES), F32),
        pad(jnp.ones((1, MLA_NOPE + MLA_ROPE), F32), LANES), jnp.zeros((1, LANES), F32),
        pad(jnp.ones((1, MLA_ROPE), F32), LANES)], axis=-1)
    return jnp.concatenate([lat, jnp.broadcast_to(ident_row, (tm, 5 * LANES))], axis=0)


def _rot_cols(w, dim):
    lead = w.shape[:-1]
    g = w.reshape(lead + (-1, 2, dim // 2))
    return jnp.concatenate([-g[..., 1:2, :], g[..., 0:1, :]], axis=-2).reshape(w.shape)


def _to_slots(w, heads, width):
    lead = w.shape[:-1]
    g = w.reshape(lead + (heads, width))
    g = jnp.pad(g, [(0, 0)] * len(lead) + [(0, 0), (0, LANES - width)])
    return g.reshape(lead + (heads * LANES,))


def _layer_weights(l, p):
    d = p["w_in"].shape[1]
    w_in = p["w_in"][l]
    da = w_in[:, :3 * DA_Q]
    wq, wk, wv = da[:, :DA_Q], da[:, DA_Q:2 * DA_Q], da[:, 2 * DA_Q:]
    hy = w_in[:, 3 * DA_Q:3 * DA_Q + HY_IN]
    mla = w_in[:, 3 * DA_Q + HY_IN:]
    w_cq = mla[:, :MLA_Q_RANK]
    w_ckv = mla[:, MLA_Q_RANK:MLA_Q_RANK + MLA_KV_RANK]
    w_kr = mla[:, MLA_Q_RANK + MLA_KV_RANK:]
    kr_seg = jnp.pad(jnp.concatenate([w_kr, _rot_cols(w_kr, MLA_ROPE // 2)], -1),
                     ((0, 0), (0, LANES - 2 * MLA_ROPE)))
    w_in_ext = jnp.concatenate([
        _to_slots(wq, DA_HEADS, DA_V_DIM), _to_slots(wk, DA_HEADS, DA_V_DIM),
        _to_slots(wv, DA_HEADS, DA_V_DIM), hy, w_cq, w_ckv, kr_seg], axis=-1).astype(BF16)

    qh = MLA_NOPE + MLA_ROPE
    wuq = p["mla_w_uq"][l].reshape(MLA_Q_RANK, MLA_HEADS, qh)
    wukv = p["mla_w_ukv"][l].reshape(MLA_KV_RANK, MLA_HEADS, MLA_NOPE + MLA_V)
    wuk = _to_slots(wukv[..., :MLA_NOPE].reshape(MLA_KV_RANK, -1), MLA_HEADS, MLA_NOPE)
    wuv = _to_slots(wukv[..., MLA_NOPE:].reshape(MLA_KV_RANK, -1), MLA_HEADS, MLA_V)
    lane_src = jnp.arange(LANES)[:, None]
    lane_dst = jnp.arange(MLA_SLOTS)[None, :] % LANES
    e2 = ((lane_src < 2 * MLA_ROPE) & (lane_dst == MLA_NOPE + lane_src % MLA_ROPE)).astype(BF16)

    slot_lane = jnp.arange(DA_SLOTS) % LANES
    vb_da = (slot_lane == DA_V_DIM).astype(F32)[None, :]
    vb_mla = ((jnp.arange(MLA_SLOTS) % LANES) == MLA_V).astype(F32)[None, :]

    w_out = p["w_out"][l]
    w_out_ext = jnp.concatenate([
        _to_slots(w_out[:DA_HEADS * DA_V_DIM].T, DA_HEADS, DA_V_DIM).T,
        w_out[DA_HEADS * DA_V_DIM:DA_HEADS * DA_V_DIM + HY_CH],
        _to_slots(w_out[DA_HEADS * DA_V_DIM + HY_CH:].T, MLA_HEADS, MLA_V).T], axis=0).astype(BF16)

    def pad_lanes(v):
        return jnp.pad(v, (0, LANES - v.shape[0]))[None, :]

    lam = jnp.concatenate([pad_lanes(p[k][l]) for k in
                           ("da_lambda_q1", "da_lambda_k1", "da_lambda_q2", "da_lambda_k2")], axis=0)
    hw = {
        "conv_w": p["hy_conv_w"][l][:, None, :], "conv_b": p["hy_conv_b"][l][None, :],
        "fw1": jnp.pad(p["hy_fw1"][l], ((0, HY_HIDDEN - HY_EMB), (0, 0))), "fb1": p["hy_fb1"][l][None, :],
        "fw2": p["hy_fw2"][l], "fb2": p["hy_fb2"][l][None, :],
        "fw3": p["hy_fw3"][l], "fb3": p["hy_fb3"][l][None, :],
        "fw4": p["hy_fw4"][l], "freq": p["hy_freq"][l][None, :], "bias": p["hy_bias"][l][:, None, :],
    }
    return {
        "w_in": w_in_ext, "qg": p["mla_q_norm_g"][l][None, :], "kvg": p["mla_kv_norm_g"][l][None, :],
        "wuq": _to_slots(wuq.reshape(MLA_Q_RANK, -1), MLA_HEADS, qh).astype(BF16),
        "wukv": jnp.concatenate([wuk, wuv], -1).astype(BF16), "e2": e2,
        "vb_da": vb_da, "vb_mla": vb_mla, "w_out": w_out_ext, "lam": lam,
        "subln_g": pad_lanes(p["da_subln_g"][l]), "hy": hw,
        "ln1_g": p["ln1_g"][l][None, :], "ln1_b": p["ln1_b"][l][None, :],
        "ln2_g": p["ln2_g"][l][None, :], "ln2_b": p["ln2_b"][l][None, :],
    }


def _hyena_consts(n):
    t = jnp.linspace(0.0, 1.0, n, dtype=F32)[:, None]
    bands = (HY_EMB - 1) // 2
    w = 2.0 * math.pi * jnp.arange(n, dtype=F32)[:, None] / n
    f = jnp.linspace(1e-4, bands - 1, bands, dtype=F32)[None, :]
    z = jnp.concatenate([t, jnp.cos(f * w), -jnp.sin(f * w)], axis=-1)
    z = jnp.pad(z, ((0, 0), (0, HY_HIDDEN - HY_EMB)))
    max_decay = math.log(HY_DECAY_TARGET) / HY_DECAY_SHORT_PCT
    min_decay = math.log(HY_DECAY_TARGET) / HY_DECAY_LONG_PCT
    deltas = jnp.linspace(min_decay, max_decay, HY_CH, dtype=F32)
    win = jnp.exp(-t * jnp.abs(deltas)[None, :]) + HY_DECAY_SHIFT
    k = jnp.arange(n, dtype=jnp.int32)
    ang = ((k[:, None] * k[None, :]) % (2 * n)).astype(F32) * (math.pi / n)
    sgn = (1 - 2 * (k % 2)).astype(F32)[:, None]
    return {"z": z, "win": win, "sgn": sgn, "c": jnp.cos(ang).astype(BF16), "s": jnp.sin(ang).astype(BF16)}


def _pick_tile(limit, *sizes):
    t = limit
    while any(s % t for s in sizes):
        t //= 2
    return t


def _make_geo(b, nl, nc):
    t_lat = b * nl
    tm = _pick_tile(512, nl, b * nc)
    tm_ffn = _pick_tile(512, nl, b * nc)
    geo = {"b": b, "nl": nl, "nc": nc, "tm": tm, "tm_ffn": tm_ffn, "fsub": 256,
           "tq": _pick_tile(1024, nl), "mb": 512, "hps_da": 4, "hps_mla": 6}
    n_lat_t, per_b = t_lat // tm, nl // tm
    geo["mod_map"] = lambda i: (jnp.where(i < n_lat_t, i // per_b, b), 0, 0)
    geo["tab_map"] = lambda i: (jnp.where(i < n_lat_t, i % per_b, per_b), 0)
    n_lat_f, per_b_f = t_lat // tm_ffn, nl // tm_ffn
    geo["mod_map_ffn"] = lambda i: (jnp.where(i < n_lat_f, i // per_b_f, b), 0, 0)
    return geo


def kernel(x, c, ctx, c_ctx, ada_w, ada_b, w_in, da_lambda_q1, da_lambda_k1, da_lambda_q2, da_lambda_k2, da_subln_g, hy_conv_w, hy_conv_b, hy_fw1, hy_fb1, hy_fw2, hy_fb2, hy_fw3, hy_fb3, hy_fw4, hy_freq, hy_bias, mla_q_norm_g, mla_w_uq, mla_kv_norm_g, mla_w_ukv, w_out, ln1_g, ln1_b, ln2_g, ln2_b, ffn_w_gate, ffn_w_up, ffn_w_down, moe_router, moe_w_gate, moe_w_up, moe_w_down):
    p = dict(w_in=w_in, da_lambda_q1=da_lambda_q1, da_lambda_k1=da_lambda_k1, da_lambda_q2=da_lambda_q2,
             da_lambda_k2=da_lambda_k2, da_subln_g=da_subln_g, hy_conv_w=hy_conv_w, hy_conv_b=hy_conv_b,
             hy_fw1=hy_fw1, hy_fb1=hy_fb1, hy_fw2=hy_fw2, hy_fb2=hy_fb2, hy_fw3=hy_fw3, hy_fb3=hy_fb3,
             hy_fw4=hy_fw4, hy_freq=hy_freq, hy_bias=hy_bias, mla_q_norm_g=mla_q_norm_g, mla_w_uq=mla_w_uq,
             mla_kv_norm_g=mla_kv_norm_g, mla_w_ukv=mla_w_ukv, w_out=w_out, ln1_g=ln1_g, ln1_b=ln1_b,
             ln2_g=ln2_g, ln2_b=ln2_b)
    b, nl, d = x.shape
    nc = ctx.shape[1]
    depth = ada_w.shape[0]
    assert nl % nc == 0 and nl % GRID_W == 0 and nc % 8 == 0
    alpha = (2 * depth) ** 0.25
    t_lat, t_all = b * nl, b * (nl + nc)

    geo = _make_geo(b, nl, nc)
    tm = geo["tm"]

    bp = -(-(b + 1) // 8) * 8
    c_all = jnp.zeros((bp, d), F32).at[:b].set(c).at[b].set(c_ctx)
    mod_all = _ada_all(c_all, ada_w, ada_b).reshape(depth, bp, 6, d)

    tab = _rope_tables(nl, tm)
    consts_lat = _hyena_consts(nl)
    consts_ctx = _hyena_consts(nc)
    xs = jnp.concatenate([x.reshape(t_lat, d), ctx.reshape(b * nc, d)], axis=0)

    moe_w = tuple(w.reshape((-1,) + w.shape[2:]).astype(BF16) for w in (moe_w_gate, moe_w_up, moe_w_down))

    for l in range(depth):
        need_ctx = l < depth - 1
        lw = _layer_weights(l, p)
        mod = mod_all[l]
        lambda_init = 0.8 - 0.6 * math.exp(-0.3 * l)

        daq, dak, dav, hy_p, mlaq, mlak, mlav = _in_proj(xs, mod, lw, tab, geo)
        da_kw = dict(heads=DA_HEADS, hps=geo["hps_da"], n_maps=2, dv=DA_V_DIM, post_scale=1.0 - lambda_init,
                     lam=lw["lam"], gain=lw["subln_g"])
        mla_kw = dict(heads=MLA_HEADS, hps=geo["hps_mla"], n_maps=1, dv=MLA_V)
        mix_lat = (_attention(daq, dak, dav, geo, lat_queries=True, **da_kw),
                   _hy_conv(hy_p, lw["hy"], _hy_filter(lw["hy"], consts_lat), consts_lat, geo, ctx=False),
                   _attention(mlaq, mlak, mlav, geo, lat_queries=True, **mla_kw))
        mix_ctx = None
        if need_ctx:
            mix_ctx = (_attention(daq, dak, dav, geo, lat_queries=False, **da_kw),
                       _hy_conv(hy_p, lw["hy"], _hy_filter(lw["hy"], consts_ctx), consts_ctx, geo, ctx=True),
                       _attention(mlaq, mlak, mlav, geo, lat_queries=False, **mla_kw))

        idx = l // 2
        if l % 2 == 0:
            x1, h2 = _out_proj(mix_lat, mix_ctx, xs, mod, lw, geo, alpha=alpha)
            xs = _ffn_dense(h2, ffn_w_gate[idx:idx + 1].astype(BF16), ffn_w_up[idx:idx + 1].astype(BF16),
                            ffn_w_down[idx:idx + 1].astype(BF16), x1, mod, lw, geo, alpha=alpha)
        else:
            router = jnp.pad(moe_router[idx], ((0, 0), (0, LANES - N_EXPERTS))).astype(BF16)
            x1, h2, route = _out_proj(mix_lat, mix_ctx, xs, mod, lw, geo, alpha=alpha, router=router)
            lw["moe_wg"], lw["moe_wu"], lw["moe_wd"] = moe_w
            lw["moe_first_expert"] = idx * N_EXPERTS
            xs = _moe_layer(x1, h2, route, mod, lw, geo, alpha=alpha)
    return xs[:t_lat].reshape(b, nl, d)
```

```python
import functools
import math

import jax
import jax.numpy as jnp
from jax import lax
from jax.experimental import pallas as pl
from jax.experimental.pallas import tpu as pltpu

F32 = jnp.float32
BF16 = jnp.bfloat16

GRID_W = 64
DA_HEADS = 4
DA_HEAD_DIM = 48
DA_V_DIM = 2 * DA_HEAD_DIM
HY_CH = 256
HY_ORDER = 2
HY_EMB = 33
HY_HIDDEN = 64
HY_DECAY_TARGET = 1e-2
HY_DECAY_SHORT_PCT = 0.3
HY_DECAY_LONG_PCT = 1.5
HY_DECAY_SHIFT = 0.05
MLA_HEADS = 6
MLA_Q_RANK = 256
MLA_KV_RANK = 128
MLA_NOPE = 64
MLA_ROPE = 32
MLA_V = 64
N_EXPERTS = 8
ROPE_BASE = 10000.0
LN_EPS = 1e-5
RMS_EPS = 1e-6

LANES = 128
TOKEN_TILE = 8
VMEM_LIMIT = 56 * 1024 * 1024

DA_Q = DA_HEADS * 2 * DA_HEAD_DIM
DA_SLOTS = DA_HEADS * LANES
MLA_SLOTS = MLA_HEADS * LANES
HY_IN = (HY_ORDER + 1) * HY_CH
MIX_EXT = DA_SLOTS + HY_CH + MLA_SLOTS

_C_Q, _C_K, _C_V = 0, 512, 1024
_C_HY = 1536
_C_CQ = _C_HY + HY_IN
_C_CKV = _C_CQ + MLA_Q_RANK
_C_KROPE = _C_CKV + MLA_KV_RANK
IN_EXT = _C_KROPE + LANES


def _cparams(sem):
    return pltpu.CompilerParams(dimension_semantics=sem, vmem_limit_bytes=VMEM_LIMIT)


def _resident(shape):
    nd = len(shape)
    return pl.BlockSpec(shape, lambda *_: (0,) * nd, pipeline_mode=pl.Buffered(1))


def _dot(a, b):
    return jnp.dot(a, b, preferred_element_type=F32)


def _dot_nt(a, b):
    return lax.dot_general(a, b, (((1,), (1,)), ((), ())), preferred_element_type=F32)


def _layer_norm_rows(y, g, b):
    mu = jnp.mean(y, axis=-1, keepdims=True)
    yc = y - mu
    var = jnp.mean(yc * yc, axis=-1, keepdims=True)
    return yc * lax.rsqrt(var + LN_EPS) * g + b


def _sigmoid(x):
    return 1.0 / (1.0 + jnp.exp(-x))


def _ada_kernel(c_ref, w_ref, b_ref, o_ref):
    c = c_ref[...]
    act = (c * _sigmoid(c)).astype(BF16)
    o_ref[0] = _dot(act, w_ref[0].astype(BF16)) + b_ref[0]


def _ada_all(c_all, ada_w, ada_b):
    depth, d, n6 = ada_w.shape
    bp = c_all.shape[0]
    tn = 512
    return pl.pallas_call(
        _ada_kernel,
        grid=(depth, n6 // tn),
        in_specs=[
            pl.BlockSpec((bp, d), lambda l, j: (0, 0)),
            pl.BlockSpec((1, d, tn), lambda l, j: (l, 0, j)),
            pl.BlockSpec((1, 1, tn), lambda l, j: (l, 0, j)),
        ],
        out_specs=pl.BlockSpec((1, bp, tn), lambda l, j: (l, 0, j)),
        out_shape=jax.ShapeDtypeStruct((depth, bp, n6), F32),
        compiler_params=_cparams(("parallel", "parallel")),
        name="ada_mod",
    )(c_all, ada_w, ada_b.reshape(depth, 1, n6))


def _tile_lanes(x, reps):
    return jnp.concatenate([x] * reps, axis=-1)


def _rms_rows(x, g):
    return x * lax.rsqrt(jnp.mean(x * x, axis=-1, keepdims=True) + RMS_EPS) * g


def _rotate_half_slots(x, slots, half, first_half):
    parts = []
    for h in range(slots):
        xs = x[:, h * LANES:(h + 1) * LANES]
        parts.append(jnp.where(first_half, pltpu.roll(xs, LANES - half, 1), pltpu.roll(xs, half, 1)))
    return jnp.concatenate(parts, axis=-1)


def _rows_of(x_refs, n_lat_tiles):
    if len(x_refs) == 1:
        return x_refs[0][...]
    return jnp.where(pl.program_id(0) < n_lat_tiles, x_refs[0][...], x_refs[1][...])


def _in_proj_kernel(*refs, da_scale, mla_scale, n_x, n_lat_tiles):
    x_refs, refs = refs[:n_x], refs[n_x:]
    (mod_ref, w_ref, tab_ref, qg_ref, kvg_ref, wuq_ref, wukv_ref, vbda_ref, vbmla_ref,
     daq_ref, dak_ref, dav_ref, hy_ref, mlaq_ref, mlak_ref, mlav_ref) = refs
    m = mod_ref[0]
    h = (_rows_of(x_refs, n_lat_tiles) * (1.0 + m[1:2]) + m[0:1]).astype(BF16)

    def seg(a, b):
        return _dot(h, w_ref[:, a:b])

    lane = lax.broadcasted_iota(jnp.int32, (h.shape[0], LANES), 1)
    da_half = DA_HEAD_DIM // 4
    da_first = (lane % (2 * da_half)) < da_half
    cos_da = _tile_lanes(tab_ref[:, 0:128], DA_HEADS)
    sin_da = _tile_lanes(tab_ref[:, 128:256], DA_HEADS)
    q = seg(_C_Q, _C_K)
    daq_ref[...] = ((q * cos_da + _rotate_half_slots(q, DA_HEADS, da_half, da_first) * sin_da)
                    * da_scale).astype(BF16)
    k = seg(_C_K, _C_V)
    dak_ref[...] = (k * cos_da + _rotate_half_slots(k, DA_HEADS, da_half, da_first) * sin_da).astype(BF16)
    dav_ref[...] = (seg(_C_V, _C_HY) + vbda_ref[...]).astype(BF16)
    hy_ref[...] = seg(_C_HY, _C_CQ)

    cqn = _rms_rows(seg(_C_CQ, _C_CKV), qg_ref[...]).astype(BF16)
    mla_half = MLA_ROPE // 4
    mla_first = ((lane - MLA_NOPE) % (2 * mla_half)) < mla_half
    cos_mq = _tile_lanes(tab_ref[:, 256:384], MLA_HEADS)
    sin_mq = _tile_lanes(tab_ref[:, 384:512], MLA_HEADS)
    mq = _dot(cqn, wuq_ref[...])
    mlaq_ref[...] = ((mq * cos_mq + _rotate_half_slots(mq, MLA_HEADS, mla_half, mla_first) * sin_mq)
                     * mla_scale).astype(BF16)

    ckvn = _rms_rows(seg(_C_CKV, _C_KROPE), kvg_ref[...]).astype(BF16)
    kv = _dot(ckvn, wukv_ref[...])
    krw = seg(_C_KROPE, IN_EXT) * tab_ref[:, 512:640]
    kr = pltpu.roll(krw + pltpu.roll(krw, LANES - MLA_ROPE, 1), MLA_NOPE, 1)
    kr = jnp.where((lane >= MLA_NOPE) & (lane < MLA_NOPE + MLA_ROPE), kr, 0.0)
    mlak_ref[...] = (kv[:, :MLA_SLOTS] + _tile_lanes(kr, MLA_HEADS)).astype(BF16)
    mlav_ref[...] = (kv[:, MLA_SLOTS:] + vbmla_ref[...]).astype(BF16)


def _x_specs(xs, tm):
    if not isinstance(xs, tuple):
        return [pl.BlockSpec((tm, xs.shape[1]), lambda i: (i, 0))], [xs], xs.shape[0], 0
    n_lat_tiles = xs[0].shape[0] // tm
    d = xs[0].shape[1]
    specs = [pl.BlockSpec((tm, d), lambda i: (jnp.minimum(i, n_lat_tiles - 1), 0)),
             pl.BlockSpec((tm, d), lambda i: (jnp.maximum(i - n_lat_tiles, 0), 0))]
    return specs, list(xs), xs[0].shape[0] + xs[1].shape[0], n_lat_tiles


def _in_proj(xs, mod, lw, tab, geo):
    tm = geo["tm"]
    x_specs, x_args, t_rows, n_lat_tiles = _x_specs(xs, tm)
    d = x_args[0].shape[1]
    row = lambda i: (i, 0)
    outs = [(DA_SLOTS, BF16), (DA_SLOTS, BF16), (DA_SLOTS, BF16), (HY_IN, F32),
            (MLA_SLOTS, BF16), (MLA_SLOTS, BF16), (MLA_SLOTS, BF16)]
    kern = functools.partial(_in_proj_kernel, da_scale=DA_HEAD_DIM ** -0.5,
                             mla_scale=(MLA_NOPE + MLA_ROPE) ** -0.5, n_x=len(x_args), n_lat_tiles=n_lat_tiles)
    return pl.pallas_call(
        kern,
        grid=(t_rows // tm,),
        in_specs=x_specs + [
            pl.BlockSpec((1, 6, d), geo["mod_map"]),
            _resident((d, IN_EXT)),
            pl.BlockSpec((tm, 5 * LANES), geo["tab_map"]),
            _resident((1, MLA_Q_RANK)),
            _resident((1, MLA_KV_RANK)),
            _resident((MLA_Q_RANK, MLA_SLOTS)),
            _resident((MLA_KV_RANK, 2 * MLA_SLOTS)),
            _resident((1, DA_SLOTS)),
            _resident((1, MLA_SLOTS)),
        ],
        out_specs=[pl.BlockSpec((tm, w), row) for w, _ in outs],
        out_shape=[jax.ShapeDtypeStruct((t_rows, w), dt) for w, dt in outs],
        compiler_params=_cparams(("parallel",)),
        name="in_proj",
    )(*x_args, mod, lw["w_in"], tab, lw["qg"], lw["kvg"], lw["wuq"], lw["wukv"],
      lw["vb_da"], lw["vb_mla"])


def _attn_kernel(*refs, n_maps, dv, post_scale, lat_queries, hps):
    if n_maps == 2:
        lam_ref, g_ref, q_ref, kl_ref, kc_ref, vl_ref, vc_ref, o_ref = refs
    else:
        q_ref, kl_ref, kc_ref, vl_ref, vc_ref, o_ref = refs
    tq = q_ref.shape[0]
    lane = lax.broadcasted_iota(jnp.int32, (tq, LANES), 1)

    def attend(qm, hs, include_lat):
        sc = _dot_nt(qm, kc_ref[:, hs])
        mx = jnp.max(sc, axis=-1, keepdims=True)
        if include_lat:
            sl = _dot_nt(qm, kl_ref[:, hs])
            mx = jnp.maximum(mx, jnp.max(sl, axis=-1, keepdims=True))
            ol = _dot(jnp.exp(sl - mx).astype(BF16), vl_ref[:, hs])
        ol_c = _dot(jnp.exp(sc - mx).astype(BF16), vc_ref[:, hs])
        ol = ol + ol_c if include_lat else ol_c
        den = jnp.sum(jnp.where(lane == dv, ol, 0.0), axis=-1, keepdims=True)
        return ol / den

    def head(h, include_lat):
        hs = slice(h * LANES, (h + 1) * LANES)
        q = q_ref[:, hs]
        if n_maps == 2:
            q1 = jnp.where(lane < DA_HEAD_DIM, q, jnp.zeros_like(q))
            q2 = jnp.where(lane < DA_HEAD_DIM, jnp.zeros_like(q), q)
            lp = lam_ref[...]
            lam = (jnp.exp(jnp.sum(lp[0:1] * lp[1:2], axis=-1, keepdims=True))
                   - jnp.exp(jnp.sum(lp[2:3] * lp[3:4], axis=-1, keepdims=True))
                   + (1.0 - post_scale))
            o = attend(q1, hs, include_lat) - lam * attend(q2, hs, include_lat)
            o = jnp.where(lane < dv, o, 0.0)
            ms = jnp.sum(o * o, axis=-1, keepdims=True) * (1.0 / dv)
            return o * lax.rsqrt(ms + RMS_EPS) * g_ref[...] * post_scale
        return jnp.where(lane < dv, attend(q, hs, include_lat), 0.0)

    o = jnp.concatenate([head(h, lat_queries) for h in range(hps)], axis=-1)
    o_ref[...] = o.astype(o_ref.dtype)


def _attention(q, k, v, geo, *, heads, hps, n_maps, dv, lat_queries, post_scale=1.0, lam=None, gain=None):
    b, nl, nc = geo["b"], geo["nl"], geo["nc"]
    tq = geo["tq"] if lat_queries else nc
    n_q_tiles = (nl if lat_queries else nc) // tq
    q_blk0 = 0 if lat_queries else b * nl // nc
    ctx_kv0 = b * nl // nc
    width = hps * LANES
    kv_lat = pl.BlockSpec((nl if lat_queries else 8, width), lambda bi, hi, j: (bi if lat_queries else 0, hi))
    kv_ctx = pl.BlockSpec((nc, width), lambda bi, hi, j: (ctx_kv0 + bi, hi))
    in_specs = [pl.BlockSpec((tq, width), lambda bi, hi, j: (q_blk0 + bi * n_q_tiles + j, hi)),
                kv_lat, kv_ctx, kv_lat, kv_ctx]
    args = [q, k, k, v, v]
    if n_maps == 2:
        in_specs = [pl.BlockSpec((4, LANES), lambda bi, hi, j: (0, 0)),
                    pl.BlockSpec((1, LANES), lambda bi, hi, j: (0, 0))] + in_specs
        args = [lam, gain] + args
    kern = functools.partial(_attn_kernel, n_maps=n_maps, dv=dv, post_scale=post_scale,
                             lat_queries=lat_queries, hps=hps)
    return pl.pallas_call(
        kern,
        grid=(b, heads // hps, n_q_tiles),
        in_specs=in_specs,
        out_specs=pl.BlockSpec((tq, width), lambda bi, hi, j: (bi * n_q_tiles + j, hi)),
        out_shape=jax.ShapeDtypeStruct((b * n_q_tiles * tq, q.shape[1]), BF16),
        compiler_params=_cparams(("parallel", "parallel", "arbitrary")),
        name=("diff_attn" if n_maps == 2 else "mla_attn") + ("_lat" if lat_queries else "_ctx"),
    )(*args)


def _hy_filter_kernel(z_ref, w1_ref, b1_ref, w2_ref, b2_ref, w3_ref, b3_ref, w4_ref, fr_ref,
                      win_ref, sgn_ref, a_ref, d_ref, hn_ref, *, rows):
    hp = lax.Precision.HIGHEST
    fr = fr_ref[...]

    def lin(x, w_ref, b_ref):
        return jnp.dot(x, w_ref[...], precision=hp, preferred_element_type=F32) + b_ref[...]

    hdn = jnp.sin(fr * lin(z_ref[...], w1_ref, b1_ref))
    hdn = jnp.sin(fr * lin(hdn, w2_ref, b2_ref))
    hdn = jnp.sin(fr * lin(hdn, w3_ref, b3_ref))
    filt = jnp.dot(hdn, w4_ref[...], precision=hp, preferred_element_type=F32)
    win = _tile_lanes(win_ref[...], HY_ORDER)
    half = HY_ORDER * HY_CH
    fwd = filt[:, :half] * win
    bwd = filt[:, half:] * win
    i = pl.program_id(0)
    row = lax.broadcasted_iota(jnp.int32, bwd.shape, 0) + i * rows
    bwd = jnp.where(row == 0, 0.0, bwd)
    a = fwd + bwd
    a_ref[...] = a.astype(BF16)
    d_ref[...] = (bwd - fwd).astype(BF16)

    @pl.when(i == 0)
    def _():
        hn_ref[...] = jnp.zeros_like(hn_ref)

    hn_ref[...] += jnp.sum(a * sgn_ref[...], axis=0, keepdims=True)


def _hy_spectrum_kernel(c_ref, s_ref, a_ref, d_ref, hr_ref, hi_ref):
    hr_ref[...] = _dot(c_ref[...], a_ref[...])
    hi_ref[...] = _dot(s_ref[...], d_ref[...])


def _hy_filter(hw, consts):
    n = consts["c"].shape[0]
    half = HY_ORDER * HY_CH
    tr = _pick_tile(512, n)
    row = lambda i: (i, 0)
    weights = (hw["fw1"], hw["fb1"], hw["fw2"], hw["fb2"], hw["fw3"], hw["fb3"], hw["fw4"], hw["freq"])
    a, d, hn = pl.pallas_call(
        functools.partial(_hy_filter_kernel, rows=tr),
        grid=(n // tr,),
        in_specs=([pl.BlockSpec((tr, HY_HIDDEN), row)] + [_resident(w.shape) for w in weights]
                  + [pl.BlockSpec((tr, HY_CH), row), pl.BlockSpec((tr, 1), row)]),
        out_specs=[pl.BlockSpec((tr, half), row), pl.BlockSpec((tr, half), row),
                   pl.BlockSpec((1, half), lambda i: (0, 0))],
        out_shape=[jax.ShapeDtypeStruct((n, half), BF16), jax.ShapeDtypeStruct((n, half), BF16),
                   jax.ShapeDtypeStruct((1, half), F32)],
        compiler_params=_cparams(("arbitrary",)),
        name="hyena_filter",
    )(consts["z"], *weights, consts["win"], consts["sgn"])
    col = lambda j: (0, j)
    hr, hi = pl.pallas_call(
        _hy_spectrum_kernel,
        grid=(half // HY_CH,),
        in_specs=[_resident((n, n)), _resident((n, n)), pl.BlockSpec((n, HY_CH), col),
                  pl.BlockSpec((n, HY_CH), col)],
        out_specs=[pl.BlockSpec((n, HY_CH), col), pl.BlockSpec((n, HY_CH), col)],
        out_shape=[jax.ShapeDtypeStruct((n, half), F32), jax.ShapeDtypeStruct((n, half), F32)],
        compiler_params=_cparams(("parallel",)),
        name="hyena_spectrum",
    )(consts["c"], consts["s"], a, d)
    return hr, hi, hn


def _hy_conv_kernel(*refs, n, kc, rc):
    (p_ref, cw_ref, cb_ref, c_ref, s_ref, hr_ref, hi_ref, hn_ref, sgn_ref, bias_ref, o_ref,
     z_s, zb_s, y_s) = refs
    ri = lax.broadcasted_iota(jnp.int32, (rc, HY_CH), 0)
    zero_row = jnp.zeros((1, HY_CH), F32)

    def short_conv(g, r0):
        cs = slice(g * HY_CH, (g + 1) * HY_CH)
        p = p_ref[r0:r0 + rc, cs]
        up = p_ref[r0 - 1:r0, cs] if r0 > 0 else zero_row
        dn = p_ref[r0 + rc:r0 + rc + 1, cs] if r0 + rc < n else zero_row
        prev = jnp.where(ri == 0, up, pltpu.roll(p, 1, 0))
        nxt = jnp.where(ri == rc - 1, dn, pltpu.roll(p, rc - 1, 0))
        return prev * cw_ref[0, :, cs] + p * cw_ref[1, :, cs] + nxt * cw_ref[2, :, cs] + cb_ref[:, cs]

    for r0 in range(0, n, rc):
        z_s[r0:r0 + rc, :] = short_conv(HY_ORDER, r0)
    for i in range(HY_ORDER):
        cs = slice(i * HY_CH, (i + 1) * HY_CH)
        zb_s[...] = z_s[...].astype(BF16)
        xn = jnp.sum(z_s[...] * sgn_ref[...], axis=0, keepdims=True)
        y_s[...] = sgn_ref[...] * (xn * hn_ref[:, cs] * (0.5 / n))
        for k0 in range(0, n, kc):
            ks = slice(k0, k0 + kc)
            zb = zb_s[...]
            a = _dot(c_ref[ks, :], zb)
            bm = _dot(s_ref[ks, :], zb)
            hr = hr_ref[ks, cs]
            hi = hi_ref[ks, cs]
            yr = a * hr + bm * hi
            if k0 == 0:
                yr = jnp.where(lax.broadcasted_iota(jnp.int32, yr.shape, 0) == 0, 0.5 * yr, yr)
            yi = a * hi - bm * hr
            y_s[...] += (_dot(c_ref[:, ks], yr.astype(BF16)) - _dot(s_ref[:, ks], yi.astype(BF16))) * (1.0 / n)
        dst = o_ref if i == HY_ORDER - 1 else z_s
        for r0 in range(0, n, rc):
            rs = slice(r0, r0 + rc)
            dst[rs, :] = short_conv(i, r0) * (y_s[rs, :] + z_s[rs, :] * bias_ref[i])


def _hy_conv(hy_p, hw, spec, consts, geo, *, ctx):
    b, nl, nc = geo["b"], geo["nl"], geo["nc"]
    n = nc if ctx else nl
    blk0 = b * nl // nc if ctx else 0
    hr, hi, hn = spec
    args = [hy_p, hw["conv_w"], hw["conv_b"], consts["c"], consts["s"], hr, hi, hn, consts["sgn"],
            hw["bias"]]
    in_specs = ([pl.BlockSpec((n, HY_IN), lambda bi: (blk0 + bi, 0))]
                + [_resident(a.shape) for a in args[1:]])
    kern = functools.partial(_hy_conv_kernel, n=n, kc=_pick_tile(512, n), rc=_pick_tile(256, n))
    return pl.pallas_call(
        kern,
        grid=(b,),
        in_specs=in_specs,
        out_specs=pl.BlockSpec((n, HY_CH), lambda bi: (bi, 0)),
        out_shape=jax.ShapeDtypeStruct((b * n, HY_CH), F32),
        scratch_shapes=[pltpu.VMEM((n, HY_CH), F32), pltpu.VMEM((n, HY_CH), BF16), pltpu.VMEM((n, HY_CH), F32)],
        compiler_params=_cparams(("parallel",)),
        name="hyena_conv_ctx" if ctx else "hyena_conv_lat",
    )(*args)


def _top2_route(logits):
    lane = lax.broadcasted_iota(jnp.int32, logits.shape, 1)
    neg = jnp.float32(-jnp.inf)
    lg = jnp.where(lane < N_EXPERTS, logits, neg)
    m1 = jnp.max(lg, axis=-1, keepdims=True)
    i1 = jnp.min(jnp.where(lg == m1, lane, LANES), axis=-1, keepdims=True)
    lg2 = jnp.where(lane == i1, neg, lg)
    m2 = jnp.max(lg2, axis=-1, keepdims=True)
    i2 = jnp.min(jnp.where(lg2 == m2, lane, LANES), axis=-1, keepdims=True)
    e = jnp.exp(m2 - m1)
    g1 = 1.0 / (1.0 + e)
    g2 = e * g1
    out = jnp.where(lane == 0, i1.astype(F32), 0.0)
    out = jnp.where(lane == 1, i2.astype(F32), out)
    out = jnp.where(lane == 2, g1, out)
    return jnp.where(lane == 3, g2, out)


def _out_proj_kernel(*refs, alpha, moe, n_lat_tiles, has_ctx, n_x):
    n_mix = 6 if has_ctx else 3
    mix_refs, x_refs, refs = refs[:n_mix], refs[n_mix:n_mix + n_x], refs[n_mix + n_x:]
    if moe:
        w_ref, mod_ref, g_ref, b_ref, rw_ref, x1_ref, h2_ref, route_ref = refs[:8]
    else:
        w_ref, mod_ref, g_ref, b_ref, x1_ref, h2_ref = refs[:6]

    def mix(da_ref, hy_ref, mla_ref):
        return (_dot(da_ref[...], w_ref[0:DA_SLOTS])
                + _dot(hy_ref[...].astype(BF16), w_ref[DA_SLOTS:DA_SLOTS + HY_CH])
                + _dot(mla_ref[...], w_ref[DA_SLOTS + HY_CH:MIX_EXT]))

    if has_ctx:
        o_ref = refs[-1]
        is_lat = pl.program_id(0) < n_lat_tiles

        @pl.when(is_lat)
        def _():
            o_ref[...] = mix(*mix_refs[:3])

        @pl.when(jnp.logical_not(is_lat))
        def _():
            o_ref[...] = mix(*mix_refs[3:])

        o = o_ref[...]
    else:
        o = mix(*mix_refs)
    m = mod_ref[0]
    x1 = _layer_norm_rows(alpha * _rows_of(x_refs, n_lat_tiles) + m[2:3] * o, g_ref[...], b_ref[...])
    x1_ref[...] = x1
    h2 = x1 * (1.0 + m[4:5]) + m[3:4]
    if moe:
        _to_token_tiles(h2_ref, h2)
        route_ref[...] = _top2_route(_dot(h2.astype(BF16), rw_ref[...]))
    else:
        h2_ref[...] = h2.astype(BF16)


def _out_proj(mix_lat, mix_ctx, xs, mod, lw, geo, *, alpha, router=None):
    tm = geo["tm"]
    x_specs, x_args, _, _ = _x_specs(xs, tm)
    d = x_args[0].shape[1]
    row = lambda i: (i, 0)
    moe = router is not None
    has_ctx = mix_ctx is not None
    n_lat_tiles = mix_lat[0].shape[0] // tm
    t_rows = mix_lat[0].shape[0] + (mix_ctx[0].shape[0] if has_ctx else 0)
    widths = (DA_SLOTS, HY_CH, MLA_SLOTS)
    lat_row = lambda i: (jnp.minimum(i, n_lat_tiles - 1), 0)
    ctx_row = lambda i: (jnp.maximum(i - n_lat_tiles, 0), 0)
    in_specs = [pl.BlockSpec((tm, w), lat_row) for w in widths]
    args = list(mix_lat)
    if has_ctx:
        in_specs += [pl.BlockSpec((tm, w), ctx_row) for w in widths]
        args += list(mix_ctx)
    in_specs += x_specs + [_resident((MIX_EXT, d)),
                           pl.BlockSpec((1, 6, d), geo["mod_map"]), _resident((1, d)), _resident((1, d))]
    args += x_args + [lw["w_out"], mod, lw["ln1_g"], lw["ln1_b"]]
    if moe:
        h2_spec = pl.BlockSpec((tm * TOKEN_TILE, LANES), row)
        h2_shape = jax.ShapeDtypeStruct((t_rows * TOKEN_TILE, LANES), F32)
    else:
        h2_spec, h2_shape = pl.BlockSpec((tm, d), row), jax.ShapeDtypeStruct((t_rows, d), BF16)
    out_specs = [pl.BlockSpec((tm, d), row), h2_spec]
    out_shape = [jax.ShapeDtypeStruct((t_rows, d), F32), h2_shape]
    if moe:
        in_specs.append(_resident((d, LANES)))
        args.append(router)
        out_specs.append(pl.BlockSpec((tm, LANES), row))
        out_shape.append(jax.ShapeDtypeStruct((t_rows, LANES), F32))
    return pl.pallas_call(
        functools.partial(_out_proj_kernel, alpha=alpha, moe=moe, n_lat_tiles=n_lat_tiles, has_ctx=has_ctx,
                          n_x=len(x_args)),
        grid=(t_rows // tm,),
        in_specs=in_specs,
        out_specs=out_specs,
        out_shape=out_shape,
        scratch_shapes=[pltpu.VMEM((tm, d), F32)] if has_ctx else [],
        compiler_params=_cparams(("parallel",)),
        name="out_proj_ln1",
    )(*args)


def _from_token_tiles(ref, rows):
    return jnp.concatenate([ref[pl.ds(c, rows, stride=TOKEN_TILE), :] for c in range(TOKEN_TILE)], axis=-1)


def _to_token_tiles(ref, val):
    rows = val.shape[0]
    for c in range(TOKEN_TILE):
        ref[pl.ds(c, rows, stride=TOKEN_TILE), :] = val[:, c * LANES:(c + 1) * LANES]


def _swiglu_kernel(*refs, alpha, grouped, fsub):
    if grouped:
        be_ref, na_ref, h_ref, wg_ref, wu_ref, wd_ref, o_ref, act_ref = refs
    else:
        h_ref, wg_ref, wu_ref, wd_ref, x_ref, mod_ref, g_ref, b_ref, o_ref, act_ref = refs

    def body():
        if grouped:
            h = _from_token_tiles(h_ref, act_ref.shape[0]).astype(BF16)
        else:
            h = h_ref[...]
        for c0 in range(0, wg_ref.shape[2], fsub):
            cs = slice(c0, c0 + fsub)
            g = _dot(h, wg_ref[0, :, cs])
            u = _dot(h, wu_ref[0, :, cs])
            act_ref[:, cs] = (g * _sigmoid(g) * u).astype(BF16)
        y = _dot(act_ref[...], wd_ref[0])
        if grouped:
            _to_token_tiles(o_ref, y)
        else:
            m = mod_ref[0]
            o_ref[...] = _layer_norm_rows(alpha * x_ref[...] + m[5:6] * y, g_ref[...], b_ref[...])

    if grouped:
        active = pl.program_id(0) < na_ref[0]
        pl.when(active)(body)

        @pl.when(jnp.logical_not(active))
        def _():
            o_ref[...] = jnp.zeros_like(o_ref)
    else:
        body()


def _ffn_dense(h2, wg, wu, wd, x1, mod, lw, geo, *, alpha):
    t_rows, d = h2.shape
    f_dim = wg.shape[-1]
    tm = geo["tm_ffn"]
    row = lambda i: (i, 0)
    return pl.pallas_call(
        functools.partial(_swiglu_kernel, alpha=alpha, grouped=False, fsub=geo["fsub"]),
        grid=(t_rows // tm,),
        in_specs=[pl.BlockSpec((tm, d), row), _resident((1, d, f_dim)), _resident((1, d, f_dim)),
                  _resident((1, f_dim, d)), pl.BlockSpec((tm, d), row),
                  pl.BlockSpec((1, 6, d), geo["mod_map_ffn"]), _resident((1, d)), _resident((1, d))],
        out_specs=pl.BlockSpec((tm, d), row),
        out_shape=jax.ShapeDtypeStruct((t_rows, d), F32),
        scratch_shapes=[pltpu.VMEM((tm, f_dim), BF16)],
        compiler_params=_cparams(("parallel",)),
        name="ffn_swiglu_ln2",
    )(h2, wg, wu, wd, x1, mod, lw["ln2_g"], lw["ln2_b"])


def _ffn_grouped(rows, wg, wu, wd, block_expert, n_active, geo, *, first_expert):
    d, f_dim = wg.shape[1], wg.shape[2]
    mb = geo["mb"]
    r_rows = rows.shape[0] // TOKEN_TILE
    blk = lambda i, be, na: (jnp.minimum(i, na[0] - 1), 0)
    wmap = lambda i, be, na: (first_expert + be[jnp.minimum(i, na[0] - 1)], 0, 0)
    grid_spec = pltpu.PrefetchScalarGridSpec(
        num_scalar_prefetch=2,
        grid=(r_rows // mb,),
        in_specs=[pl.BlockSpec((mb * TOKEN_TILE, LANES), blk),
                  pl.BlockSpec((1, d, f_dim), wmap), pl.BlockSpec((1, d, f_dim), wmap),
                  pl.BlockSpec((1, f_dim, d), wmap)],
        out_specs=pl.BlockSpec((mb * TOKEN_TILE, LANES), lambda i, be, na: (i, 0)),
        scratch_shapes=[pltpu.VMEM((mb, f_dim), BF16)],
    )
    return pl.pallas_call(
        functools.partial(_swiglu_kernel, alpha=0.0, grouped=True, fsub=geo["fsub"]),
        grid_spec=grid_spec,
        out_shape=jax.ShapeDtypeStruct(rows.shape, F32),
        compiler_params=_cparams(("arbitrary",)),
        name="moe_experts",
    )(block_expert, n_active, rows, wg, wu, wd)


def _gather_kernel(idx_ref, src_ref, o_ref, sem, *, rows):
    group = 8

    def start(g, carry):
        for j in range(group):
            r = g * group + j
            src = pl.multiple_of(idx_ref[r] * TOKEN_TILE, TOKEN_TILE)
            dst = pl.multiple_of(r * TOKEN_TILE, TOKEN_TILE)
            pltpu.make_async_copy(src_ref.at[pl.ds(src, TOKEN_TILE)], o_ref.at[pl.ds(dst, TOKEN_TILE)],
                                  sem).start(priority=j % 2)
        return carry

    lax.fori_loop(0, rows // group, start, 0)
    pltpu.make_async_copy(o_ref, o_ref, sem).wait()


def _gather_rows(src, idx, rows_per_step):
    m = idx.shape[0]
    return pl.pallas_call(
        functools.partial(_gather_kernel, rows=rows_per_step),
        grid=(m // rows_per_step,),
        in_specs=[pl.BlockSpec((rows_per_step,), lambda i: (i,), memory_space=pltpu.SMEM),
                  pl.BlockSpec(memory_space=pl.ANY)],
        out_specs=pl.BlockSpec((rows_per_step * TOKEN_TILE, LANES), lambda i: (i, 0)),
        out_shape=jax.ShapeDtypeStruct((m * TOKEN_TILE, LANES), src.dtype),
        scratch_shapes=[pltpu.SemaphoreType.DMA(())],
        compiler_params=_cparams(("arbitrary",)),
        name="gather_rows",
    )(idx, src)


def _dispatch_kernel(dest_ref, init_ref, src_ref, o_ref, sem, *, tokens):
    group = 8

    def start(g, carry):
        for j in range(group):
            a = g * group + j
            src = pl.multiple_of((g * (group // 2) + j // 2) * TOKEN_TILE, TOKEN_TILE)
            dst = pl.multiple_of(dest_ref[a] * TOKEN_TILE, TOKEN_TILE)
            pltpu.make_async_copy(src_ref.at[pl.ds(src, TOKEN_TILE)], o_ref.at[pl.ds(dst, TOKEN_TILE)],
                                  sem).start(priority=j % 2)
        return carry

    lax.fori_loop(0, 2 * tokens // group, start, 0)
    pltpu.make_async_copy(src_ref, src_ref, sem).wait()
    pltpu.make_async_copy(src_ref, src_ref, sem).wait()


def _dispatch_rows(src, dest, init, tokens_per_step):
    n_tok = src.shape[0] // TOKEN_TILE
    return pl.pallas_call(
        functools.partial(_dispatch_kernel, tokens=tokens_per_step),
        grid=(n_tok // tokens_per_step,),
        in_specs=[pl.BlockSpec((2 * tokens_per_step,), lambda i: (i,), memory_space=pltpu.SMEM),
                  pl.BlockSpec(memory_space=pl.ANY),
                  pl.BlockSpec((tokens_per_step * TOKEN_TILE, LANES), lambda i: (i, 0))],
        out_specs=pl.BlockSpec(memory_space=pl.ANY),
        out_shape=jax.ShapeDtypeStruct(init.shape, init.dtype),
        scratch_shapes=[pltpu.SemaphoreType.DMA(())],
        input_output_aliases={1: 0},
        compiler_params=_cparams(("arbitrary",)),
        name="dispatch_rows",
    )(dest, init, src)


def _moe_ln2_kernel(x_ref, y1_ref, y2_ref, route_ref, mod_ref, g_ref, b_ref, o_ref, *, alpha):
    m = mod_ref[0]
    rt = route_ref[...]
    tm = rt.shape[0]
    lane = lax.broadcasted_iota(jnp.int32, rt.shape, 1)
    g1 = jnp.sum(jnp.where(lane == 2, rt, 0.0), axis=-1, keepdims=True)
    g2 = jnp.sum(jnp.where(lane == 3, rt, 0.0), axis=-1, keepdims=True)
    f = _from_token_tiles(y1_ref, tm) * g1 + _from_token_tiles(y2_ref, tm) * g2
    o_ref[...] = _layer_norm_rows(alpha * x_ref[...] + m[5:6] * f, g_ref[...], b_ref[...])


def _moe_ln2(x1, pairs, route, mod, lw, geo, *, alpha):
    t_rows, d = x1.shape
    tm = geo["tm"]
    nt = t_rows // tm
    row = lambda i: (i, 0)
    return pl.pallas_call(
        functools.partial(_moe_ln2_kernel, alpha=alpha),
        grid=(nt,),
        in_specs=[pl.BlockSpec((tm, d), row), pl.BlockSpec((tm * TOKEN_TILE, LANES), row),
                  pl.BlockSpec((tm * TOKEN_TILE, LANES), lambda i: (nt + i, 0)), pl.BlockSpec((tm, LANES), row),
                  pl.BlockSpec((1, 6, d), geo["mod_map"]), _resident((1, d)), _resident((1, d))],
        out_specs=pl.BlockSpec((tm, d), row),
        out_shape=jax.ShapeDtypeStruct((t_rows, d), F32),
        compiler_params=_cparams(("parallel",)),
        name="moe_combine_ln2",
    )(x1, pairs, pairs, route, mod, lw["ln2_g"], lw["ln2_b"])


def _moe_layer(x1, h2_tiles, route, mod, lw, geo, *, alpha):
    t_rows = x1.shape[0]
    mb = geo["mb"]
    n_assign = 2 * t_rows
    n_blocks = -(-(n_assign + N_EXPERTS * (mb - 1)) // mb)
    expert_of = route[:, 0:2].astype(jnp.int32).reshape(n_assign)
    onehot = (expert_of[:, None] == jnp.arange(N_EXPERTS, dtype=jnp.int32)[None, :]).astype(jnp.int32)
    csum = jnp.cumsum(onehot, axis=0)
    rank = jnp.take_along_axis(csum, expert_of[:, None], axis=1)[:, 0] - 1
    counts = csum[-1]
    padded = (counts + mb - 1) // mb * mb
    padded_end = jnp.cumsum(padded)
    dest = (padded_end - padded)[expert_of] + rank
    n_active = (padded_end[-1] // mb).astype(jnp.int32).reshape(1)
    block_expert = jnp.minimum(
        jnp.searchsorted(padded_end, jnp.arange(n_blocks, dtype=jnp.int32) * mb, side="right"),
        N_EXPERTS - 1).astype(jnp.int32)
    rows = _dispatch_rows(h2_tiles, dest.astype(jnp.int32), jnp.zeros((n_blocks * mb * TOKEN_TILE, LANES), F32),
                          geo["tm"])
    out = _ffn_grouped(rows, lw["moe_wg"], lw["moe_wu"], lw["moe_wd"], block_expert, n_active, geo,
                       first_expert=lw["moe_first_expert"])
    pair_idx = jnp.concatenate([dest[0::2], dest[1::2]]).astype(jnp.int32)
    pairs = _gather_rows(out, pair_idx, geo["tm"])
    return _moe_ln2(x1, pairs, route, mod, lw, geo, alpha=alpha)


def _rope_tables(n_lat, tm):
    pos = jnp.arange(n_lat, dtype=jnp.int32)
    row = (pos // GRID_W).astype(F32)
    col = (pos % GRID_W).astype(F32)

    def axial(rot_dim):
        axis_dim = rot_dim // 2
        inv = ROPE_BASE ** (-jnp.arange(0, axis_dim, 2, dtype=F32) / axis_dim)
        ang_r = row[:, None] * inv[None, :]
        ang_c = col[:, None] * inv[None, :]
        ang = jnp.concatenate([ang_r, ang_r, ang_c, ang_c], axis=-1)
        return jnp.cos(ang), jnp.sin(ang)

    def pad(x, width, fill=0.0):
        return jnp.pad(x, ((0, 0), (0, width - x.shape[1])), constant_values=fill)

    def signed(sin, half):
        first = (jnp.arange(sin.shape[1]) % (2 * half)) < half
        return jnp.where(first[None, :], -sin, sin)

    cd, sd = axial(DA_HEAD_DIM)
    cm, sm = axial(MLA_ROPE)
    sd_s, sm_s = signed(sd, DA_HEAD_DIM // 4), signed(sm, MLA_ROPE // 4)
    ones_nope = jnp.ones((n_lat, MLA_NOPE), F32)
    zeros_nope = jnp.zeros((n_lat, MLA_NOPE), F32)
    lat = jnp.concatenate([
        pad(jnp.concatenate([cd, cd], -1), LANES), pad(jnp.concatenate([sd_s, sd_s], -1), LANES),
        pad(jnp.concatenate([ones_nope, cm], -1), LANES), pad(jnp.concatenate([zeros_nope, sm_s], -1), LANES),
        pad(jnp.concatenate([cm, sm], -1), LANES)], axis=-1)
    ident_row = jnp.concatenate([
        pad(jnp.ones((1, 2 * DA_HEAD_DIM), F32), LANES), jnp.zeros((1, LANES), F32),
        pad(jnp.ones((1, MLA_NOPE + MLA_ROPE), F32), LANES), jnp.zeros((1, LANES), F32),
        pad(jnp.ones((1, MLA_ROPE), F32), LANES)], axis=-1)
    return jnp.concatenate([lat, jnp.broadcast_to(ident_row, (tm, 5 * LANES))], axis=0)


def _rot_cols(w, dim):
    lead = w.shape[:-1]
    g = w.reshape(lead + (-1, 2, dim // 2))
    return jnp.concatenate([-g[..., 1:2, :], g[..., 0:1, :]], axis=-2).reshape(w.shape)


def _to_slots(w, heads, width):
    lead = w.shape[:-1]
    g = w.reshape(lead + (heads, width))
    g = jnp.pad(g, [(0, 0)] * len(lead) + [(0, 0), (0, LANES - width)])
    return g.reshape(lead + (heads * LANES,))


def _layer_weights(l, p):
    d = p["w_in"].shape[1]
    w_in = p["w_in"][l]
    da = w_in[:, :3 * DA_Q]
    wq, wk, wv = da[:, :DA_Q], da[:, DA_Q:2 * DA_Q], da[:, 2 * DA_Q:]
    hy = w_in[:, 3 * DA_Q:3 * DA_Q + HY_IN]
    mla = w_in[:, 3 * DA_Q + HY_IN:]
    w_cq = mla[:, :MLA_Q_RANK]
    w_ckv = mla[:, MLA_Q_RANK:MLA_Q_RANK + MLA_KV_RANK]
    w_kr = mla[:, MLA_Q_RANK + MLA_KV_RANK:]
    kr_seg = jnp.pad(jnp.concatenate([w_kr, _rot_cols(w_kr, MLA_ROPE // 2)], -1),
                     ((0, 0), (0, LANES - 2 * MLA_ROPE)))
    w_in_ext = jnp.concatenate([
        _to_slots(wq, DA_HEADS, DA_V_DIM), _to_slots(wk, DA_HEADS, DA_V_DIM),
        _to_slots(wv, DA_HEADS, DA_V_DIM), hy, w_cq, w_ckv, kr_seg], axis=-1).astype(BF16)

    qh = MLA_NOPE + MLA_ROPE
    wuq = p["mla_w_uq"][l].reshape(MLA_Q_RANK, MLA_HEADS, qh)
    wukv = p["mla_w_ukv"][l].reshape(MLA_KV_RANK, MLA_HEADS, MLA_NOPE + MLA_V)
    wuk = _to_slots(wukv[..., :MLA_NOPE].reshape(MLA_KV_RANK, -1), MLA_HEADS, MLA_NOPE)
    wuv = _to_slots(wukv[..., MLA_NOPE:].reshape(MLA_KV_RANK, -1), MLA_HEADS, MLA_V)
    slot_lane = jnp.arange(DA_SLOTS) % LANES
    vb_da = (slot_lane == DA_V_DIM).astype(F32)[None, :]
    vb_mla = ((jnp.arange(MLA_SLOTS) % LANES) == MLA_V).astype(F32)[None, :]

    w_out = p["w_out"][l]
    w_out_ext = jnp.concatenate([
        _to_slots(w_out[:DA_HEADS * DA_V_DIM].T, DA_HEADS, DA_V_DIM).T,
        w_out[DA_HEADS * DA_V_DIM:DA_HEADS * DA_V_DIM + HY_CH],
        _to_slots(w_out[DA_HEADS * DA_V_DIM + HY_CH:].T, MLA_HEADS, MLA_V).T], axis=0).astype(BF16)

    def pad_lanes(v):
        return jnp.pad(v, (0, LANES - v.shape[0]))[None, :]

    lam = jnp.concatenate([pad_lanes(p[k][l]) for k in
                           ("da_lambda_q1", "da_lambda_k1", "da_lambda_q2", "da_lambda_k2")], axis=0)
    hw = {
        "conv_w": p["hy_conv_w"][l][:, None, :], "conv_b": p["hy_conv_b"][l][None, :],
        "fw1": jnp.pad(p["hy_fw1"][l], ((0, HY_HIDDEN - HY_EMB), (0, 0))), "fb1": p["hy_fb1"][l][None, :],
        "fw2": p["hy_fw2"][l], "fb2": p["hy_fb2"][l][None, :],
        "fw3": p["hy_fw3"][l], "fb3": p["hy_fb3"][l][None, :],
        "fw4": p["hy_fw4"][l], "freq": p["hy_freq"][l][None, :], "bias": p["hy_bias"][l][:, None, :],
    }
    return {
        "w_in": w_in_ext, "qg": p["mla_q_norm_g"][l][None, :], "kvg": p["mla_kv_norm_g"][l][None, :],
        "wuq": _to_slots(wuq.reshape(MLA_Q_RANK, -1), MLA_HEADS, qh).astype(BF16),
        "wukv": jnp.concatenate([wuk, wuv], -1).astype(BF16),
        "vb_da": vb_da, "vb_mla": vb_mla, "w_out": w_out_ext, "lam": lam,
        "subln_g": pad_lanes(p["da_subln_g"][l]), "hy": hw,
        "ln1_g": p["ln1_g"][l][None, :], "ln1_b": p["ln1_b"][l][None, :],
        "ln2_g": p["ln2_g"][l][None, :], "ln2_b": p["ln2_b"][l][None, :],
    }


def _hyena_consts(n):
    t = jnp.linspace(0.0, 1.0, n, dtype=F32)[:, None]
    bands = (HY_EMB - 1) // 2
    w = 2.0 * math.pi * jnp.arange(n, dtype=F32)[:, None] / n
    f = jnp.linspace(1e-4, bands - 1, bands, dtype=F32)[None, :]
    z = jnp.concatenate([t, jnp.cos(f * w), -jnp.sin(f * w)], axis=-1)
    z = jnp.pad(z, ((0, 0), (0, HY_HIDDEN - HY_EMB)))
    max_decay = math.log(HY_DECAY_TARGET) / HY_DECAY_SHORT_PCT
    min_decay = math.log(HY_DECAY_TARGET) / HY_DECAY_LONG_PCT
    deltas = jnp.linspace(min_decay, max_decay, HY_CH, dtype=F32)
    win = jnp.exp(-t * jnp.abs(deltas)[None, :]) + HY_DECAY_SHIFT
    k = jnp.arange(n, dtype=jnp.int32)
    ang = ((k[:, None] * k[None, :]) % (2 * n)).astype(F32) * (math.pi / n)
    sgn = (1 - 2 * (k % 2)).astype(F32)[:, None]
    return {"z": z, "win": win, "sgn": sgn, "c": jnp.cos(ang).astype(BF16), "s": jnp.sin(ang).astype(BF16)}


def _pick_tile(limit, *sizes):
    t = limit
    while any(s % t for s in sizes):
        t //= 2
    return t


def _make_geo(b, nl, nc):
    t_lat = b * nl
    tm = _pick_tile(512, nl, b * nc)
    tm_ffn = _pick_tile(512, nl, b * nc)
    geo = {"b": b, "nl": nl, "nc": nc, "tm": tm, "tm_ffn": tm_ffn, "fsub": 256,
           "tq": _pick_tile(1024, nl), "mb": 512, "hps_da": 4, "hps_mla": 6}
    n_lat_t, per_b = t_lat // tm, nl // tm
    geo["mod_map"] = lambda i: (jnp.where(i < n_lat_t, i // per_b, b), 0, 0)
    geo["tab_map"] = lambda i: (jnp.where(i < n_lat_t, i % per_b, per_b), 0)
    n_lat_f, per_b_f = t_lat // tm_ffn, nl // tm_ffn
    geo["mod_map_ffn"] = lambda i: (jnp.where(i < n_lat_f, i // per_b_f, b), 0, 0)
    return geo


def kernel(x, c, ctx, c_ctx, ada_w, ada_b, w_in, da_lambda_q1, da_lambda_k1, da_lambda_q2, da_lambda_k2, da_subln_g, hy_conv_w, hy_conv_b, hy_fw1, hy_fb1, hy_fw2, hy_fb2, hy_fw3, hy_fb3, hy_fw4, hy_freq, hy_bias, mla_q_norm_g, mla_w_uq, mla_kv_norm_g, mla_w_ukv, w_out, ln1_g, ln1_b, ln2_g, ln2_b, ffn_w_gate, ffn_w_up, ffn_w_down, moe_router, moe_w_gate, moe_w_up, moe_w_down):
    p = dict(w_in=w_in, da_lambda_q1=da_lambda_q1, da_lambda_k1=da_lambda_k1, da_lambda_q2=da_lambda_q2,
             da_lambda_k2=da_lambda_k2, da_subln_g=da_subln_g, hy_conv_w=hy_conv_w, hy_conv_b=hy_conv_b,
             hy_fw1=hy_fw1, hy_fb1=hy_fb1, hy_fw2=hy_fw2, hy_fb2=hy_fb2, hy_fw3=hy_fw3, hy_fb3=hy_fb3,
             hy_fw4=hy_fw4, hy_freq=hy_freq, hy_bias=hy_bias, mla_q_norm_g=mla_q_norm_g, mla_w_uq=mla_w_uq,
             mla_kv_norm_g=mla_kv_norm_g, mla_w_ukv=mla_w_ukv, w_out=w_out, ln1_g=ln1_g, ln1_b=ln1_b,
             ln2_g=ln2_g, ln2_b=ln2_b)
    b, nl, d = x.shape
    nc = ctx.shape[1]
    depth = ada_w.shape[0]
    assert nl % nc == 0 and nl % GRID_W == 0 and nc % 8 == 0
    alpha = (2 * depth) ** 0.25
    t_lat, t_all = b * nl, b * (nl + nc)

    geo = _make_geo(b, nl, nc)
    tm = geo["tm"]

    bp = -(-(b + 1) // 8) * 8
    c_all = jnp.zeros((bp, d), F32).at[:b].set(c).at[b].set(c_ctx)
    mod_all = _ada_all(c_all, ada_w, ada_b).reshape(depth, bp, 6, d)

    tab = _rope_tables(nl, tm)
    consts_lat = _hyena_consts(nl)
    consts_ctx = _hyena_consts(nc)
    xs = (x.reshape(t_lat, d), ctx.reshape(b * nc, d))

    moe_w = tuple(w.reshape((-1,) + w.shape[2:]).astype(BF16) for w in (moe_w_gate, moe_w_up, moe_w_down))

    for l in range(depth):
        need_ctx = l < depth - 1
        lw = _layer_weights(l, p)
        mod = mod_all[l]
        lambda_init = 0.8 - 0.6 * math.exp(-0.3 * l)

        daq, dak, dav, hy_p, mlaq, mlak, mlav = _in_proj(xs, mod, lw, tab, geo)
        da_kw = dict(heads=DA_HEADS, hps=geo["hps_da"], n_maps=2, dv=DA_V_DIM, post_scale=1.0 - lambda_init,
                     lam=lw["lam"], gain=lw["subln_g"])
        mla_kw = dict(heads=MLA_HEADS, hps=geo["hps_mla"], n_maps=1, dv=MLA_V)
        mix_lat = (_attention(daq, dak, dav, geo, lat_queries=True, **da_kw),
                   _hy_conv(hy_p, lw["hy"], _hy_filter(lw["hy"], consts_lat), consts_lat, geo, ctx=False),
                   _attention(mlaq, mlak, mlav, geo, lat_queries=True, **mla_kw))
        mix_ctx = None
        if need_ctx:
            mix_ctx = (_attention(daq, dak, dav, geo, lat_queries=False, **da_kw),
                       _hy_conv(hy_p, lw["hy"], _hy_filter(lw["hy"], consts_ctx), consts_ctx, geo, ctx=True),
                       _attention(mlaq, mlak, mlav, geo, lat_queries=False, **mla_kw))

        idx = l // 2
        if l % 2 == 0:
            x1, h2 = _out_proj(mix_lat, mix_ctx, xs, mod, lw, geo, alpha=alpha)
            xs = _ffn_dense(h2, ffn_w_gate[idx:idx + 1].astype(BF16), ffn_w_up[idx:idx + 1].astype(BF16),
                            ffn_w_down[idx:idx + 1].astype(BF16), x1, mod, lw, geo, alpha=alpha)
        else:
            router = jnp.pad(moe_router[idx], ((0, 0), (0, LANES - N_EXPERTS))).astype(BF16)
            x1, h2, route = _out_proj(mix_lat, mix_ctx, xs, mod, lw, geo, alpha=alpha, router=router)
            lw["moe_wg"], lw["moe_wu"], lw["moe_wd"] = moe_w
            lw["moe_first_expert"] = idx * N_EXPERTS
            xs = _moe_layer(x1, h2, route, mod, lw, geo, alpha=alpha)
    return xs.reshape(b, nl, d)
```

```python
import functools
import math

import jax
import jax.numpy as jnp
from jax import lax
from jax.experimental import pallas as pl
from jax.experimental.pallas import tpu as pltpu

F32 = jnp.float32
BF16 = jnp.bfloat16

GRID_W = 64
DA_HEADS = 4
DA_HEAD_DIM = 48
DA_V_DIM = 2 * DA_HEAD_DIM
HY_CH = 256
HY_ORDER = 2
HY_EMB = 33
HY_HIDDEN = 64
HY_DECAY_TARGET = 1e-2
HY_DECAY_SHORT_PCT = 0.3
HY_DECAY_LONG_PCT = 1.5
HY_DECAY_SHIFT = 0.05
MLA_HEADS = 6
MLA_Q_RANK = 256
MLA_KV_RANK = 128
MLA_NOPE = 64
MLA_ROPE = 32
MLA_V = 64
N_EXPERTS = 8
ROPE_BASE = 10000.0
LN_EPS = 1e-5
RMS_EPS = 1e-6

LANES = 128
TOKEN_TILE = 8
VMEM_LIMIT = 56 * 1024 * 1024

DA_Q = DA_HEADS * 2 * DA_HEAD_DIM
DA_SLOTS = DA_HEADS * LANES
MLA_SLOTS = MLA_HEADS * LANES
HY_IN = (HY_ORDER + 1) * HY_CH
MIX_EXT = DA_SLOTS + HY_CH + MLA_SLOTS

_C_Q, _C_K, _C_V = 0, 512, 1024
_C_HY = 1536
_C_CQ = _C_HY + HY_IN
_C_CKV = _C_CQ + MLA_Q_RANK
_C_KROPE = _C_CKV + MLA_KV_RANK
IN_EXT = _C_KROPE + LANES


def _cparams(sem):
    return pltpu.CompilerParams(dimension_semantics=sem, vmem_limit_bytes=VMEM_LIMIT)


def _resident(shape):
    nd = len(shape)
    return pl.BlockSpec(shape, lambda *_: (0,) * nd, pipeline_mode=pl.Buffered(1))


def _dot(a, b):
    return jnp.dot(a, b, preferred_element_type=F32)


def _dot_nt(a, b):
    return lax.dot_general(a, b, (((1,), (1,)), ((), ())), preferred_element_type=F32)


def _layer_norm_rows(y, g, b):
    mu = jnp.mean(y, axis=-1, keepdims=True)
    yc = y - mu
    var = jnp.mean(yc * yc, axis=-1, keepdims=True)
    return yc * lax.rsqrt(var + LN_EPS) * g + b


def _sigmoid(x):
    return 1.0 / (1.0 + jnp.exp(-x))


def _ada_kernel(c_ref, w_ref, b_ref, o_ref):
    c = c_ref[...]
    act = (c * _sigmoid(c)).astype(BF16)
    o_ref[0] = _dot(act, w_ref[0].astype(BF16)) + b_ref[0]


def _ada_all(c_all, ada_w, ada_b):
    depth, d, n6 = ada_w.shape
    bp = c_all.shape[0]
    tn = 512
    return pl.pallas_call(
        _ada_kernel,
        grid=(depth, n6 // tn),
        in_specs=[
            pl.BlockSpec((bp, d), lambda l, j: (0, 0)),
            pl.BlockSpec((1, d, tn), lambda l, j: (l, 0, j)),
            pl.BlockSpec((1, 1, tn), lambda l, j: (l, 0, j)),
        ],
        out_specs=pl.BlockSpec((1, bp, tn), lambda l, j: (l, 0, j)),
        out_shape=jax.ShapeDtypeStruct((depth, bp, n6), F32),
        compiler_params=_cparams(("parallel", "parallel")),
        name="ada_mod",
    )(c_all, ada_w, ada_b.reshape(depth, 1, n6))


def _tile_lanes(x, reps):
    return jnp.concatenate([x] * reps, axis=-1)


def _rms_rows(x, g):
    return x * lax.rsqrt(jnp.mean(x * x, axis=-1, keepdims=True) + RMS_EPS) * g


def _rotate_half_slots(x, slots, half, first_half):
    parts = []
    for h in range(slots):
        xs = x[:, h * LANES:(h + 1) * LANES]
        parts.append(jnp.where(first_half, pltpu.roll(xs, LANES - half, 1), pltpu.roll(xs, half, 1)))
    return jnp.concatenate(parts, axis=-1)


def _rows_of(x_refs, n_lat_tiles):
    if len(x_refs) == 1:
        return x_refs[0][...]
    return jnp.where(pl.program_id(0) < n_lat_tiles, x_refs[0][...], x_refs[1][...])


def _in_proj_kernel(*refs, da_scale, mla_scale, n_x, n_lat_tiles):
    x_refs, refs = refs[:n_x], refs[n_x:]
    (mod_ref, w_ref, tab_ref, qg_ref, kvg_ref, wuq_ref, wukv_ref, vbda_ref, vbmla_ref,
     daq_ref, dak_ref, dav_ref, hy_ref, mlaq_ref, mlak_ref, mlav_ref) = refs
    m = mod_ref[0]
    h = (_rows_of(x_refs, n_lat_tiles) * (1.0 + m[1:2]) + m[0:1]).astype(BF16)

    def seg(a, b):
        return _dot(h, w_ref[:, a:b])

    lane = lax.broadcasted_iota(jnp.int32, (h.shape[0], LANES), 1)
    da_half = DA_HEAD_DIM // 4
    da_first = (lane % (2 * da_half)) < da_half
    cos_da = _tile_lanes(tab_ref[:, 0:128], DA_HEADS)
    sin_da = _tile_lanes(tab_ref[:, 128:256], DA_HEADS)
    q = seg(_C_Q, _C_K)
    daq_ref[...] = ((q * cos_da + _rotate_half_slots(q, DA_HEADS, da_half, da_first) * sin_da)
                    * da_scale).astype(BF16)
    k = seg(_C_K, _C_V)
    dak_ref[...] = (k * cos_da + _rotate_half_slots(k, DA_HEADS, da_half, da_first) * sin_da).astype(BF16)
    dav_ref[...] = (seg(_C_V, _C_HY) + vbda_ref[...]).astype(BF16)
    hy_ref[...] = seg(_C_HY, _C_CQ)

    cqn = _rms_rows(seg(_C_CQ, _C_CKV), qg_ref[...]).astype(BF16)
    mla_half = MLA_ROPE // 4
    mla_first = ((lane - MLA_NOPE) % (2 * mla_half)) < mla_half
    cos_mq = _tile_lanes(tab_ref[:, 256:384], MLA_HEADS)
    sin_mq = _tile_lanes(tab_ref[:, 384:512], MLA_HEADS)
    mq = _dot(cqn, wuq_ref[...])
    mlaq_ref[...] = ((mq * cos_mq + _rotate_half_slots(mq, MLA_HEADS, mla_half, mla_first) * sin_mq)
                     * mla_scale).astype(BF16)

    ckvn = _rms_rows(seg(_C_CKV, _C_KROPE), kvg_ref[...]).astype(BF16)
    kv = _dot(ckvn, wukv_ref[...])
    krw = seg(_C_KROPE, IN_EXT) * tab_ref[:, 512:640]
    kr = pltpu.roll(krw + pltpu.roll(krw, LANES - MLA_ROPE, 1), MLA_NOPE, 1)
    kr = jnp.where((lane >= MLA_NOPE) & (lane < MLA_NOPE + MLA_ROPE), kr, 0.0)
    mlak_ref[...] = (kv[:, :MLA_SLOTS] + _tile_lanes(kr, MLA_HEADS)).astype(BF16)
    mlav_ref[...] = (kv[:, MLA_SLOTS:] + vbmla_ref[...]).astype(BF16)


def _x_specs(xs, tm):
    if not isinstance(xs, tuple):
        return [pl.BlockSpec((tm, xs.shape[1]), lambda i: (i, 0))], [xs], xs.shape[0], 0
    n_lat_tiles = xs[0].shape[0] // tm
    d = xs[0].shape[1]
    specs = [pl.BlockSpec((tm, d), lambda i: (jnp.minimum(i, n_lat_tiles - 1), 0)),
             pl.BlockSpec((tm, d), lambda i: (jnp.maximum(i - n_lat_tiles, 0), 0))]
    return specs, list(xs), xs[0].shape[0] + xs[1].shape[0], n_lat_tiles


def _in_proj(xs, mod, lw, tab, geo):
    tm = geo["tm"]
    x_specs, x_args, t_rows, n_lat_tiles = _x_specs(xs, tm)
    d = x_args[0].shape[1]
    row = lambda i: (i, 0)
    outs = [(DA_SLOTS, BF16), (DA_SLOTS, BF16), (DA_SLOTS, BF16), (HY_IN, F32),
            (MLA_SLOTS, BF16), (MLA_SLOTS, BF16), (MLA_SLOTS, BF16)]
    kern = functools.partial(_in_proj_kernel, da_scale=DA_HEAD_DIM ** -0.5,
                             mla_scale=(MLA_NOPE + MLA_ROPE) ** -0.5, n_x=len(x_args), n_lat_tiles=n_lat_tiles)
    return pl.pallas_call(
        kern,
        grid=(t_rows // tm,),
        in_specs=x_specs + [
            pl.BlockSpec((1, 6, d), geo["mod_map"]),
            _resident((d, IN_EXT)),
            pl.BlockSpec((tm, 5 * LANES), geo["tab_map"]),
            _resident((1, MLA_Q_RANK)),
            _resident((1, MLA_KV_RANK)),
            _resident((MLA_Q_RANK, MLA_SLOTS)),
            _resident((MLA_KV_RANK, 2 * MLA_SLOTS)),
            _resident((1, DA_SLOTS)),
            _resident((1, MLA_SLOTS)),
        ],
        out_specs=[pl.BlockSpec((tm, w), row) for w, _ in outs],
        out_shape=[jax.ShapeDtypeStruct((t_rows, w), dt) for w, dt in outs],
        compiler_params=_cparams(("parallel",)),
        name="in_proj",
    )(*x_args, mod, lw["w_in"], tab, lw["qg"], lw["kvg"], lw["wuq"], lw["wukv"],
      lw["vb_da"], lw["vb_mla"])


def _attn_kernel(*refs, n_maps, dv, post_scale, lat_queries, hps):
    if n_maps == 2:
        lam_ref, g_ref, q_ref, kl_ref, kc_ref, vl_ref, vc_ref, o_ref = refs
    else:
        q_ref, kl_ref, kc_ref, vl_ref, vc_ref, o_ref = refs
    tq = q_ref.shape[0]
    lane = lax.broadcasted_iota(jnp.int32, (tq, LANES), 1)

    def attend(qm, hs, include_lat):
        sc = _dot_nt(qm, kc_ref[:, hs])
        mx = jnp.max(sc, axis=-1, keepdims=True)
        if include_lat:
            sl = _dot_nt(qm, kl_ref[:, hs])
            mx = jnp.maximum(mx, jnp.max(sl, axis=-1, keepdims=True))
            ol = _dot(jnp.exp(sl - mx).astype(BF16), vl_ref[:, hs])
        ol_c = _dot(jnp.exp(sc - mx).astype(BF16), vc_ref[:, hs])
        ol = ol + ol_c if include_lat else ol_c
        den = jnp.sum(jnp.where(lane == dv, ol, 0.0), axis=-1, keepdims=True)
        return ol / den

    def head(h, include_lat):
        hs = slice(h * LANES, (h + 1) * LANES)
        q = q_ref[:, hs]
        if n_maps == 2:
            q1 = jnp.where(lane < DA_HEAD_DIM, q, jnp.zeros_like(q))
            q2 = jnp.where(lane < DA_HEAD_DIM, jnp.zeros_like(q), q)
            lp = lam_ref[...]
            lam = (jnp.exp(jnp.sum(lp[0:1] * lp[1:2], axis=-1, keepdims=True))
                   - jnp.exp(jnp.sum(lp[2:3] * lp[3:4], axis=-1, keepdims=True))
                   + (1.0 - post_scale))
            o = attend(q1, hs, include_lat) - lam * attend(q2, hs, include_lat)
            o = jnp.where(lane < dv, o, 0.0)
            ms = jnp.sum(o * o, axis=-1, keepdims=True) * (1.0 / dv)
            return o * lax.rsqrt(ms + RMS_EPS) * g_ref[...] * post_scale
        return jnp.where(lane < dv, attend(q, hs, include_lat), 0.0)

    o = jnp.concatenate([head(h, lat_queries) for h in range(hps)], axis=-1)
    o_ref[...] = o.astype(o_ref.dtype)


def _attention(q, k, v, geo, *, heads, hps, n_maps, dv, lat_queries, post_scale=1.0, lam=None, gain=None):
    b, nl, nc = geo["b"], geo["nl"], geo["nc"]
    tq = geo["tq"] if lat_queries else nc
    n_q_tiles = (nl if lat_queries else nc) // tq
    q_blk0 = 0 if lat_queries else b * nl // nc
    ctx_kv0 = b * nl // nc
    width = hps * LANES
    kv_lat = pl.BlockSpec((nl if lat_queries else 8, width), lambda bi, hi, j: (bi if lat_queries else 0, hi))
    kv_ctx = pl.BlockSpec((nc, width), lambda bi, hi, j: (ctx_kv0 + bi, hi))
    in_specs = [pl.BlockSpec((tq, width), lambda bi, hi, j: (q_blk0 + bi * n_q_tiles + j, hi)),
                kv_lat, kv_ctx, kv_lat, kv_ctx]
    args = [q, k, k, v, v]
    if n_maps == 2:
        in_specs = [pl.BlockSpec((4, LANES), lambda bi, hi, j: (0, 0)),
                    pl.BlockSpec((1, LANES), lambda bi, hi, j: (0, 0))] + in_specs
        args = [lam, gain] + args
    kern = functools.partial(_attn_kernel, n_maps=n_maps, dv=dv, post_scale=post_scale,
                             lat_queries=lat_queries, hps=hps)
    return pl.pallas_call(
        kern,
        grid=(b, heads // hps, n_q_tiles),
        in_specs=in_specs,
        out_specs=pl.BlockSpec((tq, width), lambda bi, hi, j: (bi * n_q_tiles + j, hi)),
        out_shape=jax.ShapeDtypeStruct((b * n_q_tiles * tq, q.shape[1]), BF16),
        compiler_params=_cparams(("parallel", "parallel", "arbitrary")),
        name=("diff_attn" if n_maps == 2 else "mla_attn") + ("_lat" if lat_queries else "_ctx"),
    )(*args)


def _hy_filter_kernel(z_ref, w1_ref, b1_ref, w2_ref, b2_ref, w3_ref, b3_ref, w4_ref, fr_ref,
                      win_ref, sgn_ref, a_ref, d_ref, hn_ref, *, rows):
    hp = lax.Precision.HIGHEST
    fr = fr_ref[...]

    def lin(x, w_ref, b_ref):
        return jnp.dot(x, w_ref[...], precision=hp, preferred_element_type=F32) + b_ref[...]

    hdn = jnp.sin(fr * lin(z_ref[...], w1_ref, b1_ref))
    hdn = jnp.sin(fr * lin(hdn, w2_ref, b2_ref))
    hdn = jnp.sin(fr * lin(hdn, w3_ref, b3_ref))
    filt = jnp.dot(hdn, w4_ref[...], precision=hp, preferred_element_type=F32)
    win = _tile_lanes(win_ref[...], HY_ORDER)
    half = HY_ORDER * HY_CH
    fwd = filt[:, :half] * win
    bwd = filt[:, half:] * win
    i = pl.program_id(0)
    row = lax.broadcasted_iota(jnp.int32, bwd.shape, 0) + i * rows
    bwd = jnp.where(row == 0, 0.0, bwd)
    a = fwd + bwd
    a_ref[...] = a.astype(BF16)
    d_ref[...] = (bwd - fwd).astype(BF16)

    @pl.when(i == 0)
    def _():
        hn_ref[...] = jnp.zeros_like(hn_ref)

    hn_ref[...] += jnp.sum(a * sgn_ref[...], axis=0, keepdims=True)


def _hy_spectrum_kernel(c_ref, s_ref, a_ref, d_ref, hr_ref, hi_ref):
    hr_ref[...] = _dot(c_ref[...], a_ref[...])
    hi_ref[...] = _dot(s_ref[...], d_ref[...])


def _hy_filter(hw, consts):
    n = consts["c"].shape[0]
    half = HY_ORDER * HY_CH
    tr = _pick_tile(512, n)
    row = lambda i: (i, 0)
    weights = (hw["fw1"], hw["fb1"], hw["fw2"], hw["fb2"], hw["fw3"], hw["fb3"], hw["fw4"], hw["freq"])
    a, d, hn = pl.pallas_call(
        functools.partial(_hy_filter_kernel, rows=tr),
        grid=(n // tr,),
        in_specs=([pl.BlockSpec((tr, HY_HIDDEN), row)] + [_resident(w.shape) for w in weights]
                  + [pl.BlockSpec((tr, HY_CH), row), pl.BlockSpec((tr, 1), row)]),
        out_specs=[pl.BlockSpec((tr, half), row), pl.BlockSpec((tr, half), row),
                   pl.BlockSpec((1, half), lambda i: (0, 0))],
        out_shape=[jax.ShapeDtypeStruct((n, half), BF16), jax.ShapeDtypeStruct((n, half), BF16),
                   jax.ShapeDtypeStruct((1, half), F32)],
        compiler_params=_cparams(("arbitrary",)),
        name="hyena_filter",
    )(consts["z"], *weights, consts["win"], consts["sgn"])
    col = lambda j: (0, j)
    hr, hi = pl.pallas_call(
        _hy_spectrum_kernel,
        grid=(half // HY_CH,),
        in_specs=[_resident((n, n)), _resident((n, n)), pl.BlockSpec((n, HY_CH), col),
                  pl.BlockSpec((n, HY_CH), col)],
        out_specs=[pl.BlockSpec((n, HY_CH), col), pl.BlockSpec((n, HY_CH), col)],
        out_shape=[jax.ShapeDtypeStruct((n, half), F32), jax.ShapeDtypeStruct((n, half), F32)],
        compiler_params=_cparams(("parallel",)),
        name="hyena_spectrum",
    )(consts["c"], consts["s"], a, d)
    return hr, hi, hn


def _hy_conv_kernel(*refs, n, kc, rc):
    (p_ref, cw_ref, cb_ref, c_ref, s_ref, hr_ref, hi_ref, hn_ref, sgn_ref, bias_ref, o_ref,
     z_s, zb_s, y_s) = refs
    ri = lax.broadcasted_iota(jnp.int32, (rc, HY_CH), 0)
    zero_row = jnp.zeros((1, HY_CH), F32)

    def short_conv(g, r0):
        cs = slice(g * HY_CH, (g + 1) * HY_CH)
        p = p_ref[r0:r0 + rc, cs]
        up = p_ref[r0 - 1:r0, cs] if r0 > 0 else zero_row
        dn = p_ref[r0 + rc:r0 + rc + 1, cs] if r0 + rc < n else zero_row
        prev = jnp.where(ri == 0, up, pltpu.roll(p, 1, 0))
        nxt = jnp.where(ri == rc - 1, dn, pltpu.roll(p, rc - 1, 0))
        return prev * cw_ref[0, :, cs] + p * cw_ref[1, :, cs] + nxt * cw_ref[2, :, cs] + cb_ref[:, cs]

    for r0 in range(0, n, rc):
        z_s[r0:r0 + rc, :] = short_conv(HY_ORDER, r0)
    for i in range(HY_ORDER):
        cs = slice(i * HY_CH, (i + 1) * HY_CH)
        zb_s[...] = z_s[...].astype(BF16)
        xn = jnp.sum(z_s[...] * sgn_ref[...], axis=0, keepdims=True)
        y_s[...] = sgn_ref[...] * (xn * hn_ref[:, cs] * (0.5 / n))
        for k0 in range(0, n, kc):
            ks = slice(k0, k0 + kc)
            zb = zb_s[...]
            a = _dot(c_ref[ks, :], zb)
            bm = _dot(s_ref[ks, :], zb)
            hr = hr_ref[ks, cs]
            hi = hi_ref[ks, cs]
            yr = a * hr + bm * hi
            if k0 == 0:
                yr = jnp.where(lax.broadcasted_iota(jnp.int32, yr.shape, 0) == 0, 0.5 * yr, yr)
            yi = a * hi - bm * hr
            y_s[...] += (_dot(c_ref[:, ks], yr.astype(BF16)) - _dot(s_ref[:, ks], yi.astype(BF16))) * (1.0 / n)
        dst = o_ref if i == HY_ORDER - 1 else z_s
        for r0 in range(0, n, rc):
            rs = slice(r0, r0 + rc)
            dst[rs, :] = short_conv(i, r0) * (y_s[rs, :] + z_s[rs, :] * bias_ref[i])


def _hy_conv(hy_p, hw, spec, consts, geo, *, ctx):
    b, nl, nc = geo["b"], geo["nl"], geo["nc"]
    n = nc if ctx else nl
    blk0 = b * nl // nc if ctx else 0
    hr, hi, hn = spec
    args = [hy_p, hw["conv_w"], hw["conv_b"], consts["c"], consts["s"], hr, hi, hn, consts["sgn"],
            hw["bias"]]
    in_specs = ([pl.BlockSpec((n, HY_IN), lambda bi: (blk0 + bi, 0))]
                + [_resident(a.shape) for a in args[1:]])
    kern = functools.partial(_hy_conv_kernel, n=n, kc=_pick_tile(512, n), rc=_pick_tile(256, n))
    return pl.pallas_call(
        kern,
        grid=(b,),
        in_specs=in_specs,
        out_specs=pl.BlockSpec((n, HY_CH), lambda bi: (bi, 0)),
        out_shape=jax.ShapeDtypeStruct((b * n, HY_CH), F32),
        scratch_shapes=[pltpu.VMEM((n, HY_CH), F32), pltpu.VMEM((n, HY_CH), BF16), pltpu.VMEM((n, HY_CH), F32)],
        compiler_params=_cparams(("parallel",)),
        name="hyena_conv_ctx" if ctx else "hyena_conv_lat",
    )(*args)


def _top2_route(logits):
    lane = lax.broadcasted_iota(jnp.int32, logits.shape, 1)
    neg = jnp.float32(-jnp.inf)
    lg = jnp.where(lane < N_EXPERTS, logits, neg)
    m1 = jnp.max(lg, axis=-1, keepdims=True)
    i1 = jnp.min(jnp.where(lg == m1, lane, LANES), axis=-1, keepdims=True)
    lg2 = jnp.where(lane == i1, neg, lg)
    m2 = jnp.max(lg2, axis=-1, keepdims=True)
    i2 = jnp.min(jnp.where(lg2 == m2, lane, LANES), axis=-1, keepdims=True)
    e = jnp.exp(m2 - m1)
    g1 = 1.0 / (1.0 + e)
    g2 = e * g1
    out = jnp.where(lane == 0, i1.astype(F32), 0.0)
    out = jnp.where(lane == 1, i2.astype(F32), out)
    out = jnp.where(lane == 2, g1, out)
    return jnp.where(lane == 3, g2, out)


def _out_proj_kernel(*refs, alpha, moe, n_lat_tiles, has_ctx, n_x):
    n_mix = 6 if has_ctx else 3
    mix_refs, x_refs, refs = refs[:n_mix], refs[n_mix:n_mix + n_x], refs[n_mix + n_x:]
    if moe:
        w_ref, mod_ref, g_ref, b_ref, rw_ref, x1_ref, h2_ref, route_ref = refs[:8]
    else:
        w_ref, mod_ref, g_ref, b_ref, x1_ref, h2_ref = refs[:6]

    def mix(da_ref, hy_ref, mla_ref):
        return (_dot(da_ref[...], w_ref[0:DA_SLOTS])
                + _dot(hy_ref[...].astype(BF16), w_ref[DA_SLOTS:DA_SLOTS + HY_CH])
                + _dot(mla_ref[...], w_ref[DA_SLOTS + HY_CH:MIX_EXT]))

    if has_ctx:
        o_ref = refs[-1]
        is_lat = pl.program_id(0) < n_lat_tiles

        @pl.when(is_lat)
        def _():
            o_ref[...] = mix(*mix_refs[:3])

        @pl.when(jnp.logical_not(is_lat))
        def _():
            o_ref[...] = mix(*mix_refs[3:])

        o = o_ref[...]
    else:
        o = mix(*mix_refs)
    m = mod_ref[0]
    x1 = _layer_norm_rows(alpha * _rows_of(x_refs, n_lat_tiles) + m[2:3] * o, g_ref[...], b_ref[...])
    x1_ref[...] = x1
    h2 = x1 * (1.0 + m[4:5]) + m[3:4]
    if moe:
        _to_token_tiles(h2_ref, h2)
        route_ref[...] = _top2_route(_dot(h2.astype(BF16), rw_ref[...]))
    else:
        h2_ref[...] = h2.astype(BF16)


def _out_proj(mix_lat, mix_ctx, xs, mod, lw, geo, *, alpha, router=None):
    tm = geo["tm"]
    x_specs, x_args, _, _ = _x_specs(xs, tm)
    d = x_args[0].shape[1]
    row = lambda i: (i, 0)
    moe = router is not None
    has_ctx = mix_ctx is not None
    n_lat_tiles = mix_lat[0].shape[0] // tm
    t_rows = mix_lat[0].shape[0] + (mix_ctx[0].shape[0] if has_ctx else 0)
    widths = (DA_SLOTS, HY_CH, MLA_SLOTS)
    lat_row = lambda i: (jnp.minimum(i, n_lat_tiles - 1), 0)
    ctx_row = lambda i: (jnp.maximum(i - n_lat_tiles, 0), 0)
    in_specs = [pl.BlockSpec((tm, w), lat_row) for w in widths]
    args = list(mix_lat)
    if has_ctx:
        in_specs += [pl.BlockSpec((tm, w), ctx_row) for w in widths]
        args += list(mix_ctx)
    in_specs += x_specs + [_resident((MIX_EXT, d)),
                           pl.BlockSpec((1, 6, d), geo["mod_map"]), _resident((1, d)), _resident((1, d))]
    args += x_args + [lw["w_out"], mod, lw["ln1_g"], lw["ln1_b"]]
    if moe:
        h2_spec = pl.BlockSpec((tm * TOKEN_TILE, LANES), row)
        h2_shape = jax.ShapeDtypeStruct((t_rows * TOKEN_TILE, LANES), F32)
    else:
        h2_spec, h2_shape = pl.BlockSpec((tm, d), row), jax.ShapeDtypeStruct((t_rows, d), BF16)
    out_specs = [pl.BlockSpec((tm, d), row), h2_spec]
    out_shape = [jax.ShapeDtypeStruct((t_rows, d), F32), h2_shape]
    if moe:
        in_specs.append(_resident((d, LANES)))
        args.append(router)
        out_specs.append(pl.BlockSpec((tm, LANES), row))
        out_shape.append(jax.ShapeDtypeStruct((t_rows, LANES), F32))
    return pl.pallas_call(
        functools.partial(_out_proj_kernel, alpha=alpha, moe=moe, n_lat_tiles=n_lat_tiles, has_ctx=has_ctx,
                          n_x=len(x_args)),
        grid=(t_rows // tm,),
        in_specs=in_specs,
        out_specs=out_specs,
        out_shape=out_shape,
        scratch_shapes=[pltpu.VMEM((tm, d), F32)] if has_ctx else [],
        compiler_params=_cparams(("parallel",)),
        name="out_proj_ln1",
    )(*args)


def _from_token_tiles(ref, rows):
    return jnp.concatenate([ref[pl.ds(c, rows, stride=TOKEN_TILE), :] for c in range(TOKEN_TILE)], axis=-1)


def _to_token_tiles(ref, val):
    rows = val.shape[0]
    for c in range(TOKEN_TILE):
        ref[pl.ds(c, rows, stride=TOKEN_TILE), :] = val[:, c * LANES:(c + 1) * LANES]


def _swiglu_kernel(*refs, alpha, grouped, fsub):
    if grouped:
        be_ref, na_ref, h_ref, wg_ref, wu_ref, wd_ref, o_ref, act_ref = refs
    else:
        h_ref, wg_ref, wu_ref, wd_ref, x_ref, mod_ref, g_ref, b_ref, o_ref, act_ref = refs

    def body():
        if grouped:
            h = _from_token_tiles(h_ref, act_ref.shape[0]).astype(BF16)
        else:
            h = h_ref[...]
        for c0 in range(0, wg_ref.shape[2], fsub):
            cs = slice(c0, c0 + fsub)
            g = _dot(h, wg_ref[0, :, cs])
            u = _dot(h, wu_ref[0, :, cs])
            act_ref[:, cs] = (g * _sigmoid(g) * u).astype(BF16)
        y = _dot(act_ref[...], wd_ref[0])
        if grouped:
            _to_token_tiles(o_ref, y)
        else:
            m = mod_ref[0]
            o_ref[...] = _layer_norm_rows(alpha * x_ref[...] + m[5:6] * y, g_ref[...], b_ref[...])

    if grouped:
        active = pl.program_id(0) < na_ref[0]
        pl.when(active)(body)

        @pl.when(jnp.logical_not(active))
        def _():
            o_ref[...] = jnp.zeros_like(o_ref)
    else:
        body()


def _ffn_dense(h2, wg, wu, wd, x1, mod, lw, geo, *, alpha):
    t_rows, d = h2.shape
    f_dim = wg.shape[-1]
    tm = geo["tm_ffn"]
    row = lambda i: (i, 0)
    return pl.pallas_call(
        functools.partial(_swiglu_kernel, alpha=alpha, grouped=False, fsub=geo["fsub"]),
        grid=(t_rows // tm,),
        in_specs=[pl.BlockSpec((tm, d), row), _resident((1, d, f_dim)), _resident((1, d, f_dim)),
                  _resident((1, f_dim, d)), pl.BlockSpec((tm, d), row),
                  pl.BlockSpec((1, 6, d), geo["mod_map_ffn"]), _resident((1, d)), _resident((1, d))],
        out_specs=pl.BlockSpec((tm, d), row),
        out_shape=jax.ShapeDtypeStruct((t_rows, d), F32),
        scratch_shapes=[pltpu.VMEM((tm, f_dim), BF16)],
        compiler_params=_cparams(("parallel",)),
        name="ffn_swiglu_ln2",
    )(h2, wg, wu, wd, x1, mod, lw["ln2_g"], lw["ln2_b"])


def _ffn_grouped(rows, wg, wu, wd, block_expert, n_active, geo, *, first_expert):
    d, f_dim = wg.shape[1], wg.shape[2]
    mb = geo["mb"]
    r_rows = rows.shape[0] // TOKEN_TILE
    blk = lambda i, be, na: (jnp.minimum(i, na[0] - 1), 0)
    wmap = lambda i, be, na: (first_expert + be[jnp.minimum(i, na[0] - 1)], 0, 0)
    grid_spec = pltpu.PrefetchScalarGridSpec(
        num_scalar_prefetch=2,
        grid=(r_rows // mb,),
        in_specs=[pl.BlockSpec((mb * TOKEN_TILE, LANES), blk),
                  pl.BlockSpec((1, d, f_dim), wmap), pl.BlockSpec((1, d, f_dim), wmap),
                  pl.BlockSpec((1, f_dim, d), wmap)],
        out_specs=pl.BlockSpec((mb * TOKEN_TILE, LANES), lambda i, be, na: (i, 0)),
        scratch_shapes=[pltpu.VMEM((mb, f_dim), BF16)],
    )
    return pl.pallas_call(
        functools.partial(_swiglu_kernel, alpha=0.0, grouped=True, fsub=geo["fsub"]),
        grid_spec=grid_spec,
        out_shape=jax.ShapeDtypeStruct(rows.shape, F32),
        compiler_params=_cparams(("arbitrary",)),
        name="moe_experts",
    )(block_expert, n_active, rows, wg, wu, wd)


def _gather_kernel(idx_ref, src_ref, o_ref, sem, *, rows):
    group = 8

    def start(g, carry):
        for j in range(group):
            r = g * group + j
            src = pl.multiple_of(idx_ref[r] * TOKEN_TILE, TOKEN_TILE)
            dst = pl.multiple_of(r * TOKEN_TILE, TOKEN_TILE)
            pltpu.make_async_copy(src_ref.at[pl.ds(src, TOKEN_TILE)], o_ref.at[pl.ds(dst, TOKEN_TILE)],
                                  sem).start(priority=j % 2)
        return carry

    lax.fori_loop(0, rows // group, start, 0)
    pltpu.make_async_copy(o_ref, o_ref, sem).wait()


def _gather_rows(src, idx, rows_per_step):
    m = idx.shape[0]
    return pl.pallas_call(
        functools.partial(_gather_kernel, rows=rows_per_step),
        grid=(m // rows_per_step,),
        in_specs=[pl.BlockSpec((rows_per_step,), lambda i: (i,), memory_space=pltpu.SMEM),
                  pl.BlockSpec(memory_space=pl.ANY)],
        out_specs=pl.BlockSpec((rows_per_step * TOKEN_TILE, LANES), lambda i: (i, 0)),
        out_shape=jax.ShapeDtypeStruct((m * TOKEN_TILE, LANES), src.dtype),
        scratch_shapes=[pltpu.SemaphoreType.DMA(())],
        compiler_params=_cparams(("arbitrary",)),
        name="gather_rows",
    )(idx, src)


def _dispatch_kernel(dest_ref, padblk_ref, padflag_ref, src_ref, o_ref, zero_ref, sem, zsem, *, tokens):
    group = 8
    blk_rows = zero_ref.shape[0]

    @pl.when(pl.program_id(0) == 0)
    def _():
        zero_ref[...] = jnp.zeros_like(zero_ref)

        def zero_copy(k):
            row0 = pl.multiple_of(padblk_ref[k] * blk_rows, blk_rows)
            return pltpu.make_async_copy(zero_ref, o_ref.at[pl.ds(row0, blk_rows)], zsem)

        for k in range(padblk_ref.shape[0]):
            pl.when(padflag_ref[k] != 0)(lambda k=k: zero_copy(k).start())
        for k in range(padblk_ref.shape[0]):
            pl.when(padflag_ref[k] != 0)(lambda k=k: zero_copy(k).wait())

    def start(g, carry):
        for j in range(group):
            a = g * group + j
            src = pl.multiple_of((g * (group // 2) + j // 2) * TOKEN_TILE, TOKEN_TILE)
            dst = pl.multiple_of(dest_ref[a] * TOKEN_TILE, TOKEN_TILE)
            pltpu.make_async_copy(src_ref.at[pl.ds(src, TOKEN_TILE)], o_ref.at[pl.ds(dst, TOKEN_TILE)],
                                  sem).start(priority=j % 2)
        return carry

    lax.fori_loop(0, 2 * tokens // group, start, 0)
    pltpu.make_async_copy(src_ref, src_ref, sem).wait()
    pltpu.make_async_copy(src_ref, src_ref, sem).wait()


def _dispatch_rows(src, dest, pad_blocks, pad_flags, n_rows, block_rows, tokens_per_step):
    n_tok = src.shape[0] // TOKEN_TILE
    n_pad = pad_blocks.shape[0]
    return pl.pallas_call(
        functools.partial(_dispatch_kernel, tokens=tokens_per_step),
        grid=(n_tok // tokens_per_step,),
        in_specs=[pl.BlockSpec((2 * tokens_per_step,), lambda i: (i,), memory_space=pltpu.SMEM),
                  pl.BlockSpec((n_pad,), lambda i: (0,), memory_space=pltpu.SMEM),
                  pl.BlockSpec((n_pad,), lambda i: (0,), memory_space=pltpu.SMEM),
                  pl.BlockSpec((tokens_per_step * TOKEN_TILE, LANES), lambda i: (i, 0))],
        out_specs=pl.BlockSpec(memory_space=pl.ANY),
        out_shape=jax.ShapeDtypeStruct((n_rows * TOKEN_TILE, LANES), src.dtype),
        scratch_shapes=[pltpu.VMEM((block_rows * TOKEN_TILE, LANES), src.dtype),
                        pltpu.SemaphoreType.DMA(()), pltpu.SemaphoreType.DMA(())],
        compiler_params=_cparams(("arbitrary",)),
        name="dispatch_rows",
    )(dest, pad_blocks, pad_flags, src)


def _moe_ln2_kernel(x_ref, y1_ref, y2_ref, route_ref, mod_ref, g_ref, b_ref, o_ref, *, alpha):
    m = mod_ref[0]
    rt = route_ref[...]
    tm = rt.shape[0]
    lane = lax.broadcasted_iota(jnp.int32, rt.shape, 1)
    g1 = jnp.sum(jnp.where(lane == 2, rt, 0.0), axis=-1, keepdims=True)
    g2 = jnp.sum(jnp.where(lane == 3, rt, 0.0), axis=-1, keepdims=True)
    f = _from_token_tiles(y1_ref, tm) * g1 + _from_token_tiles(y2_ref, tm) * g2
    o_ref[...] = _layer_norm_rows(alpha * x_ref[...] + m[5:6] * f, g_ref[...], b_ref[...])


def _moe_ln2(x1, pairs, route, mod, lw, geo, *, alpha):
    t_rows, d = x1.shape
    tm = geo["tm"]
    nt = t_rows // tm
    row = lambda i: (i, 0)
    return pl.pallas_call(
        functools.partial(_moe_ln2_kernel, alpha=alpha),
        grid=(nt,),
        in_specs=[pl.BlockSpec((tm, d), row), pl.BlockSpec((tm * TOKEN_TILE, LANES), row),
                  pl.BlockSpec((tm * TOKEN_TILE, LANES), lambda i: (nt + i, 0)), pl.BlockSpec((tm, LANES), row),
                  pl.BlockSpec((1, 6, d), geo["mod_map"]), _resident((1, d)), _resident((1, d))],
        out_specs=pl.BlockSpec((tm, d), row),
        out_shape=jax.ShapeDtypeStruct((t_rows, d), F32),
        compiler_params=_cparams(("parallel",)),
        name="moe_combine_ln2",
    )(x1, pairs, pairs, route, mod, lw["ln2_g"], lw["ln2_b"])


def _moe_layer(x1, h2_tiles, route, mod, lw, geo, *, alpha):
    t_rows = x1.shape[0]
    mb = geo["mb"]
    n_assign = 2 * t_rows
    n_blocks = -(-(n_assign + N_EXPERTS * (mb - 1)) // mb)
    expert_of = route[:, 0:2].astype(jnp.int32).reshape(n_assign)
    onehot = (expert_of[:, None] == jnp.arange(N_EXPERTS, dtype=jnp.int32)[None, :]).astype(jnp.int32)
    csum = jnp.cumsum(onehot, axis=0)
    rank = jnp.take_along_axis(csum, expert_of[:, None], axis=1)[:, 0] - 1
    counts = csum[-1]
    padded = (counts + mb - 1) // mb * mb
    padded_end = jnp.cumsum(padded)
    dest = (padded_end - padded)[expert_of] + rank
    n_active = (padded_end[-1] // mb).astype(jnp.int32).reshape(1)
    block_expert = jnp.minimum(
        jnp.searchsorted(padded_end, jnp.arange(n_blocks, dtype=jnp.int32) * mb, side="right"),
        N_EXPERTS - 1).astype(jnp.int32)
    trail = n_active[0] + jnp.arange(N_EXPERTS, dtype=jnp.int32)
    pad_blocks = jnp.concatenate([jnp.maximum(padded_end // mb - 1, 0), jnp.minimum(trail, n_blocks - 1)])
    pad_flags = jnp.concatenate([padded > counts, trail < n_blocks])
    rows = _dispatch_rows(h2_tiles, dest.astype(jnp.int32), pad_blocks.astype(jnp.int32),
                          pad_flags.astype(jnp.int32), n_blocks * mb, mb, geo["tm"])
    out = _ffn_grouped(rows, lw["moe_wg"], lw["moe_wu"], lw["moe_wd"], block_expert, n_active, geo,
                       first_expert=lw["moe_first_expert"])
    pair_idx = jnp.concatenate([dest[0::2], dest[1::2]]).astype(jnp.int32)
    pairs = _gather_rows(out, pair_idx, geo["tm"])
    return _moe_ln2(x1, pairs, route, mod, lw, geo, alpha=alpha)


def _rope_tables(n_lat, tm):
    pos = jnp.arange(n_lat, dtype=jnp.int32)
    row = (pos // GRID_W).astype(F32)
    col = (pos % GRID_W).astype(F32)

    def axial(rot_dim):
        axis_dim = rot_dim // 2
        inv = ROPE_BASE ** (-jnp.arange(0, axis_dim, 2, dtype=F32) / axis_dim)
        ang_r = row[:, None] * inv[None, :]
        ang_c = col[:, None] * inv[None, :]
        ang = jnp.concatenate([ang_r, ang_r, ang_c, ang_c], axis=-1)
        return jnp.cos(ang), jnp.sin(ang)

    def pad(x, width, fill=0.0):
        return jnp.pad(x, ((0, 0), (0, width - x.shape[1])), constant_values=fill)

    def signed(sin, half):
        first = (jnp.arange(sin.shape[1]) % (2 * half)) < half
        return jnp.where(first[None, :], -sin, sin)

    cd, sd = axial(DA_HEAD_DIM)
    cm, sm = axial(MLA_ROPE)
    sd_s, sm_s = signed(sd, DA_HEAD_DIM // 4), signed(sm, MLA_ROPE // 4)
    ones_nope = jnp.ones((n_lat, MLA_NOPE), F32)
    zeros_nope = jnp.zeros((n_lat, MLA_NOPE), F32)
    lat = jnp.concatenate([
        pad(jnp.concatenate([cd, cd], -1), LANES), pad(jnp.concatenate([sd_s, sd_s], -1), LANES),
        pad(jnp.concatenate([ones_nope, cm], -1), LANES), pad(jnp.concatenate([zeros_nope, sm_s], -1), LANES),
        pad(jnp.concatenate([cm, sm], -1), LANES)], axis=-1)
    ident_row = jnp.concatenate([
        pad(jnp.ones((1, 2 * DA_HEAD_DIM), F32), LANES), jnp.zeros((1, LANES), F32),
        pad(jnp.ones((1, MLA_NOPE + MLA_ROPE), F32), LANES), jnp.zeros((1, LANES), F32),
        pad(jnp.ones((1, MLA_ROPE), F32), LANES)], axis=-1)
    return jnp.concatenate([lat, jnp.broadcast_to(ident_row, (tm, 5 * LANES))], axis=0)


def _rot_cols(w, dim):
    lead = w.shape[:-1]
    g = w.reshape(lead + (-1, 2, dim // 2))
    return jnp.concatenate([-g[..., 1:2, :], g[..., 0:1, :]], axis=-2).reshape(w.shape)


def _to_slots(w, heads, width):
    lead = w.shape[:-1]
    g = w.reshape(lead + (heads, width))
    g = jnp.pad(g, [(0, 0)] * len(lead) + [(0, 0), (0, LANES - width)])
    return g.reshape(lead + (heads * LANES,))


def _layer_weights(l, p):
    d = p["w_in"].shape[1]
    w_in = p["w_in"][l]
    da = w_in[:, :3 * DA_Q]
    wq, wk, wv = da[:, :DA_Q], da[:, DA_Q:2 * DA_Q], da[:, 2 * DA_Q:]
    hy = w_in[:, 3 * DA_Q:3 * DA_Q + HY_IN]
    mla = w_in[:, 3 * DA_Q + HY_IN:]
    w_cq = mla[:, :MLA_Q_RANK]
    w_ckv = mla[:, MLA_Q_RANK:MLA_Q_RANK + MLA_KV_RANK]
    w_kr = mla[:, MLA_Q_RANK + MLA_KV_RANK:]
    kr_seg = jnp.pad(jnp.concatenate([w_kr, _rot_cols(w_kr, MLA_ROPE // 2)], -1),
                     ((0, 0), (0, LANES - 2 * MLA_ROPE)))
    w_in_ext = jnp.concatenate([
        _to_slots(wq, DA_HEADS, DA_V_DIM), _to_slots(wk, DA_HEADS, DA_V_DIM),
        _to_slots(wv, DA_HEADS, DA_V_DIM), hy, w_cq, w_ckv, kr_seg], axis=-1).astype(BF16)

    qh = MLA_NOPE + MLA_ROPE
    wuq = p["mla_w_uq"][l].reshape(MLA_Q_RANK, MLA_HEADS, qh)
    wukv = p["mla_w_ukv"][l].reshape(MLA_KV_RANK, MLA_HEADS, MLA_NOPE + MLA_V)
    wuk = _to_slots(wukv[..., :MLA_NOPE].reshape(MLA_KV_RANK, -1), MLA_HEADS, MLA_NOPE)
    wuv = _to_slots(wukv[..., MLA_NOPE:].reshape(MLA_KV_RANK, -1), MLA_HEADS, MLA_V)
    slot_lane = jnp.arange(DA_SLOTS) % LANES
    vb_da = (slot_lane == DA_V_DIM).astype(F32)[None, :]
    vb_mla = ((jnp.arange(MLA_SLOTS) % LANES) == MLA_V).astype(F32)[None, :]

    w_out = p["w_out"][l]
    w_out_ext = jnp.concatenate([
        _to_slots(w_out[:DA_HEADS * DA_V_DIM].T, DA_HEADS, DA_V_DIM).T,
        w_out[DA_HEADS * DA_V_DIM:DA_HEADS * DA_V_DIM + HY_CH],
        _to_slots(w_out[DA_HEADS * DA_V_DIM + HY_CH:].T, MLA_HEADS, MLA_V).T], axis=0).astype(BF16)

    def pad_lanes(v):
        return jnp.pad(v, (0, LANES - v.shape[0]))[None, :]

    lam = jnp.concatenate([pad_lanes(p[k][l]) for k in
                           ("da_lambda_q1", "da_lambda_k1", "da_lambda_q2", "da_lambda_k2")], axis=0)
    hw = {
        "conv_w": p["hy_conv_w"][l][:, None, :], "conv_b": p["hy_conv_b"][l][None, :],
        "fw1": jnp.pad(p["hy_fw1"][l], ((0, HY_HIDDEN - HY_EMB), (0, 0))), "fb1": p["hy_fb1"][l][None, :],
        "fw2": p["hy_fw2"][l], "fb2": p["hy_fb2"][l][None, :],
        "fw3": p["hy_fw3"][l], "fb3": p["hy_fb3"][l][None, :],
        "fw4": p["hy_fw4"][l], "freq": p["hy_freq"][l][None, :], "bias": p["hy_bias"][l][:, None, :],
    }
    return {
        "w_in": w_in_ext, "qg": p["mla_q_norm_g"][l][None, :], "kvg": p["mla_kv_norm_g"][l][None, :],
        "wuq": _to_slots(wuq.reshape(MLA_Q_RANK, -1), MLA_HEADS, qh).astype(BF16),
        "wukv": jnp.concatenate([wuk, wuv], -1).astype(BF16),
        "vb_da": vb_da, "vb_mla": vb_mla, "w_out": w_out_ext, "lam": lam,
        "subln_g": pad_lanes(p["da_subln_g"][l]), "hy": hw,
        "ln1_g": p["ln1_g"][l][None, :], "ln1_b": p["ln1_b"][l][None, :],
        "ln2_g": p["ln2_g"][l][None, :], "ln2_b": p["ln2_b"][l][None, :],
    }


def _hyena_consts(n):
    t = jnp.linspace(0.0, 1.0, n, dtype=F32)[:, None]
    bands = (HY_EMB - 1) // 2
    w = 2.0 * math.pi * jnp.arange(n, dtype=F32)[:, None] / n
    f = jnp.linspace(1e-4, bands - 1, bands, dtype=F32)[None, :]
    z = jnp.concatenate([t, jnp.cos(f * w), -jnp.sin(f * w)], axis=-1)
    z = jnp.pad(z, ((0, 0), (0, HY_HIDDEN - HY_EMB)))
    max_decay = math.log(HY_DECAY_TARGET) / HY_DECAY_SHORT_PCT
    min_decay = math.log(HY_DECAY_TARGET) / HY_DECAY_LONG_PCT
    deltas = jnp.linspace(min_decay, max_decay, HY_CH, dtype=F32)
    win = jnp.exp(-t * jnp.abs(deltas)[None, :]) + HY_DECAY_SHIFT
    k = jnp.arange(n, dtype=jnp.int32)
    ang = ((k[:, None] * k[None, :]) % (2 * n)).astype(F32) * (math.pi / n)
    sgn = (1 - 2 * (k % 2)).astype(F32)[:, None]
    return {"z": z, "win": win, "sgn": sgn, "c": jnp.cos(ang).astype(BF16), "s": jnp.sin(ang).astype(BF16)}


def _pick_tile(limit, *sizes):
    t = limit
    while any(s % t for s in sizes):
        t //= 2
    return t


def _make_geo(b, nl, nc):
    t_lat = b * nl
    tm = _pick_tile(512, nl, b * nc)
    tm_ffn = _pick_tile(512, nl, b * nc)
    geo = {"b": b, "nl": nl, "nc": nc, "tm": tm, "tm_ffn": tm_ffn, "fsub": 256,
           "tq": _pick_tile(1024, nl), "mb": 512, "hps_da": 4, "hps_mla": 6}
    n_lat_t, per_b = t_lat // tm, nl // tm
    geo["mod_map"] = lambda i: (jnp.where(i < n_lat_t, i // per_b, b), 0, 0)
    geo["tab_map"] = lambda i: (jnp.where(i < n_lat_t, i % per_b, per_b), 0)
    n_lat_f, per_b_f = t_lat // tm_ffn, nl // tm_ffn
    geo["mod_map_ffn"] = lambda i: (jnp.where(i < n_lat_f, i // per_b_f, b), 0, 0)
    return geo


def kernel(x, c, ctx, c_ctx, ada_w, ada_b, w_in, da_lambda_q1, da_lambda_k1, da_lambda_q2, da_lambda_k2, da_subln_g, hy_conv_w, hy_conv_b, hy_fw1, hy_fb1, hy_fw2, hy_fb2, hy_fw3, hy_fb3, hy_fw4, hy_freq, hy_bias, mla_q_norm_g, mla_w_uq, mla_kv_norm_g, mla_w_ukv, w_out, ln1_g, ln1_b, ln2_g, ln2_b, ffn_w_gate, ffn_w_up, ffn_w_down, moe_router, moe_w_gate, moe_w_up, moe_w_down):
    p = dict(w_in=w_in, da_lambda_q1=da_lambda_q1, da_lambda_k1=da_lambda_k1, da_lambda_q2=da_lambda_q2,
             da_lambda_k2=da_lambda_k2, da_subln_g=da_subln_g, hy_conv_w=hy_conv_w, hy_conv_b=hy_conv_b,
             hy_fw1=hy_fw1, hy_fb1=hy_fb1, hy_fw2=hy_fw2, hy_fb2=hy_fb2, hy_fw3=hy_fw3, hy_fb3=hy_fb3,
             hy_fw4=hy_fw4, hy_freq=hy_freq, hy_bias=hy_bias, mla_q_norm_g=mla_q_norm_g, mla_w_uq=mla_w_uq,
             mla_kv_norm_g=mla_kv_norm_g, mla_w_ukv=mla_w_ukv, w_out=w_out, ln1_g=ln1_g, ln1_b=ln1_b,
             ln2_g=ln2_g, ln2_b=ln2_b)
    b, nl, d = x.shape
    nc = ctx.shape[1]
    depth = ada_w.shape[0]
    assert nl % nc == 0 and nl % GRID_W == 0 and nc % 8 == 0
    alpha = (2 * depth) ** 0.25
    t_lat, t_all = b * nl, b * (nl + nc)

    geo = _make_geo(b, nl, nc)
    tm = geo["tm"]

    bp = -(-(b + 1) // 8) * 8
    c_all = jnp.zeros((bp, d), F32).at[:b].set(c).at[b].set(c_ctx)
    mod_all = _ada_all(c_all, ada_w, ada_b).reshape(depth, bp, 6, d)

    tab = _rope_tables(nl, tm)
    consts_lat = _hyena_consts(nl)
    consts_ctx = _hyena_consts(nc)
    xs = (x.reshape(t_lat, d), ctx.reshape(b * nc, d))

    moe_w = tuple(w.reshape((-1,) + w.shape[2:]).astype(BF16) for w in (moe_w_gate, moe_w_up, moe_w_down))

    for l in range(depth):
        need_ctx = l < depth - 1
        lw = _layer_weights(l, p)
        mod = mod_all[l]
        lambda_init = 0.8 - 0.6 * math.exp(-0.3 * l)

        daq, dak, dav, hy_p, mlaq, mlak, mlav = _in_proj(xs, mod, lw, tab, geo)
        da_kw = dict(heads=DA_HEADS, hps=geo["hps_da"], n_maps=2, dv=DA_V_DIM, post_scale=1.0 - lambda_init,
                     lam=lw["lam"], gain=lw["subln_g"])
        mla_kw = dict(heads=MLA_HEADS, hps=geo["hps_mla"], n_maps=1, dv=MLA_V)
        mix_lat = (_attention(daq, dak, dav, geo, lat_queries=True, **da_kw),
                   _hy_conv(hy_p, lw["hy"], _hy_filter(lw["hy"], consts_lat), consts_lat, geo, ctx=False),
                   _attention(mlaq, mlak, mlav, geo, lat_queries=True, **mla_kw))
        mix_ctx = None
        if need_ctx:
            mix_ctx = (_attention(daq, dak, dav, geo, lat_queries=False, **da_kw),
                       _hy_conv(hy_p, lw["hy"], _hy_filter(lw["hy"], consts_ctx), consts_ctx, geo, ctx=True),
                       _attention(mlaq, mlak, mlav, geo, lat_queries=False, **mla_kw))

        idx = l // 2
        if l % 2 == 0:
            x1, h2 = _out_proj(mix_lat, mix_ctx, xs, mod, lw, geo, alpha=alpha)
            xs = _ffn_dense(h2, ffn_w_gate[idx:idx + 1].astype(BF16), ffn_w_up[idx:idx + 1].astype(BF16),
                            ffn_w_down[idx:idx + 1].astype(BF16), x1, mod, lw, geo, alpha=alpha)
        else:
            router = jnp.pad(moe_router[idx], ((0, 0), (0, LANES - N_EXPERTS))).astype(BF16)
            x1, h2, route = _out_proj(mix_lat, mix_ctx, xs, mod, lw, geo, alpha=alpha, router=router)
            lw["moe_wg"], lw["moe_wu"], lw["moe_wd"] = moe_w
            lw["moe_first_expert"] = idx * N_EXPERTS
            xs = _moe_layer(x1, h2, route, mod, lw, geo, alpha=alpha)
    return xs.reshape(b, nl, d)
```
